```python
import math
import jax, jax.numpy as jnp
from jax import lax
import numpy as np

D_MODEL = 1024
BATCH = 2
SEQ = 8192
DEPTH = 1
DEC_BATCH = 128
DEC_SEQ = 4
PAST_LEN = 2048
PAGE_SIZE = 128

MIX_W = D_MODEL
MLSTM_W = MIX_W // 2
MLSTM_HEADS = 4
MLSTM_DH = MLSTM_W // MLSTM_HEADS
ATTN_W = MIX_W - MLSTM_W
ATTN_HEADS = 8
ATTN_DH = ATTN_W // ATTN_HEADS
DILATED = ((128, 1), (512, 4), (2048, 16))
WINDOW_MAX = 2048
BLK = 128
CONV_W = 4
CHUNK = 64
MEM_LEN = 256
MEM_HEADS = 4
MEM_DH = D_MODEL // MEM_HEADS
D_FF = -(-8 * D_MODEL // (3 * 256)) * 256
EPS = 1e-6
_WIDTHS = (MLSTM_W, MLSTM_W, MLSTM_W, MLSTM_W, MLSTM_HEADS, MLSTM_HEADS, ATTN_W, ATTN_W, ATTN_W)
SPLIT_POINTS = tuple(int(s) for s in np.cumsum(_WIDTHS)[:-1])
N_IN = int(sum(_WIDTHS))

kernel_name = 'hymba_mlstm_dilated_swa_memxattn_step'


def rmsnorm(x, g):
    xf = x.astype(jnp.float32)
    y = xf * lax.rsqrt(jnp.mean(xf * xf, axis=-1, keepdims=True) + EPS)
    return (y * g.astype(jnp.float32)).astype(x.dtype)


def headwise_layernorm(h, g):
    mu = jnp.mean(h, axis=-1, keepdims=True)
    var = jnp.mean(jnp.square(h - mu), axis=-1, keepdims=True)
    return (h - mu) * lax.rsqrt(var + EPS) * g.astype(jnp.float32).reshape(MLSTM_HEADS, MLSTM_DH)


def softmax_lse(s):
    mx = jnp.max(s, axis=-1, keepdims=True)
    e = jnp.exp(s - mx)
    den = jnp.sum(e, axis=-1, keepdims=True)
    return e / den, (mx + jnp.log(den))[..., 0]


def causal_conv(u, buf, w, b):
    T = u.shape[1]
    up = jnp.concatenate([buf.astype(u.dtype), u], axis=1)
    y = b
    for j in range(CONV_W):
        y = y + up[:, j:j + T] * w[j]
    return y, up[:, -(CONV_W - 1):]


def mlstm_chunkwise(q, k, v, i_pre, logf, C0, n0, m0):
    B, T, H, E = q.shape
    L = math.gcd(T, CHUNK)
    nc = T // L
    causal = jnp.tril(jnp.ones((L, L), dtype=bool))

    def chunks(a):
        return jnp.moveaxis(a.reshape((B, nc, L) + a.shape[2:]), 1, 0)

    def step(carry, inp):
        C, n, m = carry
        qc, kc, vc, ic, fc = inp
        b = jnp.cumsum(fc, axis=1)
        a = b + m[:, None, :]
        dmat = b[:, :, None, :] - b[:, None, :, :] + ic[:, None, :, :]
        dmat = jnp.where(causal[None, :, :, None], dmat, -jnp.inf)
        mt = jnp.maximum(a, jnp.max(dmat, axis=2))
        wa = jnp.exp(a - mt)
        sc = jnp.einsum('bthe,bshe->btsh', qc, kc) * jnp.exp(dmat - mt[:, :, None, :])
        num = wa[..., None] * jnp.einsum('bthd,bhde->bthe', qc, C) + jnp.einsum('btsh,bshe->bthe', sc, vc)
        den = wa * jnp.einsum('bthd,bhd->bth', qc, n) + jnp.sum(sc, axis=2)
        h = num / jnp.maximum(jnp.abs(den), jnp.exp(-mt))[..., None]
        bl = b[:, -1]
        g = bl[:, None, :] - b + ic
        m_new = jnp.maximum(bl + m, jnp.max(g, axis=1))
        wc = jnp.exp(bl + m - m_new)
        ws = jnp.exp(g - m_new[:, None, :])
        C_new = wc[..., None, None] * C + jnp.einsum('bsh,bshd,bshe->bhde', ws, kc, vc)
        n_new = wc[..., None] * n + jnp.einsum('bsh,bshd->bhd', ws, kc)
        return (C_new, n_new, m_new), h

    (C, n, m), hs = lax.scan(step, (C0, n0, m0), tuple(chunks(a) for a in (q, k, v, i_pre, logf)))
    h = jnp.moveaxis(hs, 0, 1).reshape(B, T, H, E)
    return h, C, n, m


def combine_branches(outs, lses):
    w = jax.nn.softmax(jnp.stack(lses, axis=0), axis=0)
    return jnp.einsum('nbth,nbthe->bthe', w, jnp.stack(outs, axis=0))


def to_residue_blocks(a, d, n, npad, front):
    B, T, H, E = a.shape
    a = a.reshape(B, n, d, H, E).transpose(0, 2, 1, 3, 4)
    return jnp.pad(a, ((0, 0), (0, 0), (front, npad - n), (0, 0), (0, 0)))


def dilated_attention_prompt(q, k, v):
    B, T, H, E = q.shape
    scale = E ** -0.5
    outs, lses = [], []
    for window, d in DILATED:
        rel = window // d
        n = T // d
        nb = -(-n // BLK)
        npad = nb * BLK
        qs = to_residue_blocks(q, d, n, npad, 0).reshape(B, d, nb, BLK, H, E)
        ks = to_residue_blocks(k, d, n, npad, BLK).reshape(B, d, nb + 1, BLK, H, E)
        vs = to_residue_blocks(v, d, n, npad, BLK).reshape(B, d, nb + 1, BLK, H, E)
        kb = jnp.concatenate([ks[:, :, :-1], ks[:, :, 1:]], axis=3)
        vb = jnp.concatenate([vs[:, :, :-1], vs[:, :, 1:]], axis=3)
        s = jnp.einsum('bdnqhe,bdnkhe->bdnhqk', qs, kb) * scale
        qi = jnp.arange(BLK)[:, None]
        kj = jnp.arange(2 * BLK)[None, :]
        dist = qi + BLK - kj
        kpos = jnp.arange(nb)[:, None, None] * BLK + kj[None] - BLK
        valid = ((dist >= 0) & (dist <= rel))[None] & (kpos >= 0)
        s = jnp.where(valid[None, None, :, None], s, -jnp.inf)
        p, lse = softmax_lse(s)
        o = jnp.einsum('bdnhqk,bdnkhe->bdnqhe', p, vb).reshape(B, d, npad, H, E)[:, :, :n]
        outs.append(o.transpose(0, 2, 1, 3, 4).reshape(B, T, H, E))
        lse = lse.transpose(0, 1, 2, 4, 3).reshape(B, d, npad, H)[:, :, :n]
        lses.append(lse.transpose(0, 2, 1, 3).reshape(B, T, H))
    return combine_branches(outs, lses)


def dilated_attention_step(q, k_new, v_new, k_buf, v_buf):
    B, T, H, E = q.shape
    wb = k_buf.shape[1]
    scale = E ** -0.5
    kk = jnp.concatenate([k_buf.astype(jnp.float32), k_new], axis=1)
    vv = jnp.concatenate([v_buf.astype(jnp.float32), v_new], axis=1)
    t = jnp.arange(T)
    outs, lses = [], []
    for window, d in DILATED:
        r = jnp.arange(window // d + 1)
        idx = wb + t[:, None] - d * r[None, :]
        valid = idx >= 0
        idx = jnp.maximum(idx, 0)
        kg = kk[:, idx]
        vg = vv[:, idx]
        s = jnp.einsum('bthe,btrhe->bthr', q, kg) * scale
        s = jnp.where(valid[None, :, None, :], s, -jnp.inf)
        p, lse = softmax_lse(s)
        outs.append(jnp.einsum('bthr,btrhe->bthe', p, vg))
        lses.append(lse)
    return combine_branches(outs, lses)


def mixing_sublayer(x, conv_buf, C0, n0, m0, k_buf, v_buf, g_mix, w_in, conv_w, conv_b, b_gates, g_mh, w_out):
    B, T, _ = x.shape
    f32 = jnp.float32
    h = rmsnorm(x, g_mix)
    qm, km, vm, om, ig, fg, qa, ka, va = jnp.split(h @ w_in, SPLIT_POINTS, axis=-1)
    qk, new_conv = causal_conv(jnp.concatenate([qm, km], axis=-1), conv_buf, conv_w, conv_b)
    qk = jax.nn.silu(qk).astype(f32)
    qm, km = jnp.split(qk, 2, axis=-1)
    qm = qm.reshape(B, T, MLSTM_HEADS, MLSTM_DH)
    km = km.reshape(B, T, MLSTM_HEADS, MLSTM_DH) * (MLSTM_DH ** -0.5)
    vm = vm.astype(f32).reshape(B, T, MLSTM_HEADS, MLSTM_DH)
    gates = jnp.concatenate([ig, fg], axis=-1).astype(f32) + b_gates.astype(f32)
    i_pre, f_pre = jnp.split(gates, 2, axis=-1)
    logf = jax.nn.log_sigmoid(f_pre)
    hm, C, n, m = mlstm_chunkwise(qm, km, vm, i_pre, logf, C0.astype(f32), n0.astype(f32), m0.astype(f32))
    hm = headwise_layernorm(hm, g_mh).reshape(B, T, MLSTM_W) * jax.nn.sigmoid(om.astype(f32))
    qa = qa.astype(f32).reshape(B, T, ATTN_HEADS, ATTN_DH)
    ka = ka.astype(f32).reshape(B, T, ATTN_HEADS, ATTN_DH)
    va = va.astype(f32).reshape(B, T, ATTN_HEADS, ATTN_DH)
    if k_buf is None:
        ha = dilated_attention_prompt(qa, ka, va)
        keep = min(WINDOW_MAX, T)
        k_rows, v_rows = ka[:, T - keep:], va[:, T - keep:]
    else:
        ha = dilated_attention_step(qa, ka, va, k_buf, v_buf)
        k_rows, v_rows = ka, va
    mix = jnp.concatenate([hm, ha.reshape(B, T, ATTN_W)], axis=-1).astype(x.dtype)
    return x + mix @ w_out, (k_rows, v_rows, new_conv, C, n, m)


def memory_kv(mem, g_mem, w_mk, w_mv):
    B = mem.shape[0]
    h = rmsnorm(mem, g_mem)
    return ((h @ w_mk).reshape(B, MEM_LEN, MEM_HEADS, MEM_DH),
            (h @ w_mv).reshape(B, MEM_LEN, MEM_HEADS, MEM_DH))


def memory_xattn(x, mem_k, mem_v, g_xattn, w_mq, w_mo):
    B, T, _ = x.shape
    q = (rmsnorm(x, g_xattn) @ w_mq).astype(jnp.float32).reshape(B, T, MEM_HEADS, MEM_DH)
    s = jnp.einsum('bthe,bmhe->bhtm', q, mem_k.astype(jnp.float32)) * (MEM_DH ** -0.5)
    p = jax.nn.softmax(s, axis=-1)
    o = jnp.einsum('bhtm,bmhe->bthe', p, mem_v.astype(jnp.float32)).reshape(B, T, D_MODEL)
    return x + o.astype(x.dtype) @ w_mo


def swiglu_ffn(x, g_ffn, w_gate, w_up, w_down):
    h = rmsnorm(x, g_ffn)
    return x + (jax.nn.silu(h @ w_gate) * (h @ w_up)) @ w_down


def decoder_layer(x, mem_k, mem_v, conv_buf, C0, n0, m0, k_buf, v_buf, g_mix, w_in, conv_w, conv_b, b_gates,
                  g_mh, w_out, g_xattn, w_mq, w_mo, g_ffn, w_gate, w_up, w_down):
    x, st = mixing_sublayer(x, conv_buf, C0, n0, m0, k_buf, v_buf, g_mix, w_in, conv_w, conv_b, b_gates, g_mh, w_out)
    x = memory_xattn(x, mem_k, mem_v, g_xattn, w_mq, w_mo)
    x = swiglu_ffn(x, g_ffn, w_gate, w_up, w_down)
    return x, st


def setup_inputs(seed: int = 0) -> dict:
    key = jax.random.key(seed)
    ks = jax.random.split(key, 32)
    nrm = lambda i, shape, s=1.0: jax.random.normal(ks[i], shape, jnp.float32) * s
    wb = min(WINDOW_MAX, PAST_LEN)
    b_gates = jnp.concatenate([nrm(20, (MLSTM_HEADS,), 0.1),
                               jnp.linspace(3.0, 6.0, MLSTM_HEADS) + nrm(21, (MLSTM_HEADS,), 0.1)])
    return {
        'x_prompt': nrm(0, (BATCH, SEQ, D_MODEL)),
        'x_sample': nrm(1, (DEC_BATCH, DEC_SEQ, D_MODEL)),
        'mem_prompt': nrm(2, (BATCH, MEM_LEN, D_MODEL)),
        'cache_attn_k': nrm(3, (DEC_BATCH, wb, ATTN_HEADS, ATTN_DH)),
        'cache_attn_v': nrm(4, (DEC_BATCH, wb, ATTN_HEADS, ATTN_DH)),
        'cache_mem_k': nrm(5, (DEC_BATCH, MEM_LEN, MEM_HEADS, MEM_DH)),
        'cache_mem_v': nrm(6, (DEC_BATCH, MEM_LEN, MEM_HEADS, MEM_DH)),
        'state_conv': nrm(7, (DEC_BATCH, CONV_W - 1, 2 * MLSTM_W)),
        'state_C': nrm(8, (DEC_BATCH, MLSTM_HEADS, MLSTM_DH, MLSTM_DH), 0.1),
        'state_n': nrm(9, (DEC_BATCH, MLSTM_HEADS, MLSTM_DH), 0.1),
        'state_m': nrm(10, (DEC_BATCH, MLSTM_HEADS)),
        'g_mix': 1.0 + nrm(11, (D_MODEL,), 0.01),
        'w_in': nrm(12, (D_MODEL, N_IN), D_MODEL ** -0.5),
        'conv_w': nrm(13, (CONV_W, 2 * MLSTM_W), CONV_W ** -0.5),
        'conv_b': nrm(14, (2 * MLSTM_W,), 0.01),
        'b_gates': b_gates,
        'g_mh': 1.0 + nrm(15, (MLSTM_W,), 0.01),
        'w_out': nrm(16, (MIX_W, D_MODEL), MIX_W ** -0.5),
        'g_mem': 1.0 + nrm(17, (D_MODEL,), 0.01),
        'w_mk': nrm(18, (D_MODEL, D_MODEL), D_MODEL ** -0.5),
        'w_mv': nrm(19, (D_MODEL, D_MODEL), D_MODEL ** -0.5),
        'g_xattn': 1.0 + nrm(22, (D_MODEL,), 0.01),
        'w_mq': nrm(23, (D_MODEL, D_MODEL), D_MODEL ** -0.5),
        'w_mo': nrm(24, (D_MODEL, D_MODEL), D_MODEL ** -0.5),
        'g_ffn': 1.0 + nrm(25, (D_MODEL,), 0.01),
        'w_gate': nrm(26, (D_MODEL, D_FF), D_MODEL ** -0.5),
        'w_up': nrm(27, (D_MODEL, D_FF), D_MODEL ** -0.5),
        'w_down': nrm(28, (D_FF, D_MODEL), D_FF ** -0.5),
        'g_final': 1.0 + nrm(29, (D_MODEL,), 0.01),
    }


def reference(x_prompt, x_sample, mem_prompt, cache_attn_k, cache_attn_v, cache_mem_k, cache_mem_v,
              state_conv, state_C, state_n, state_m, g_mix, w_in, conv_w, conv_b, b_gates, g_mh, w_out,
              g_mem, w_mk, w_mv, g_xattn, w_mq, w_mo, g_ffn, w_gate, w_up, w_down, g_final):
    f32 = jnp.float32
    B = x_prompt.shape[0]
    p_mem_k, p_mem_v = memory_kv(mem_prompt, g_mem, w_mk, w_mv)
    conv0 = jnp.zeros((B, CONV_W - 1, 2 * MLSTM_W), x_prompt.dtype)
    C0 = jnp.zeros((B, MLSTM_HEADS, MLSTM_DH, MLSTM_DH), f32)
    n0 = jnp.zeros((B, MLSTM_HEADS, MLSTM_DH), f32)
    m0 = jnp.zeros((B, MLSTM_HEADS), f32)
    hp = x_prompt
    for _ in range(DEPTH):
        hp, (p_attn_k, p_attn_v, p_conv, p_C, p_n, p_m) = decoder_layer(
            hp, p_mem_k, p_mem_v, conv0, C0, n0, m0, None, None, g_mix, w_in, conv_w, conv_b, b_gates,
            g_mh, w_out, g_xattn, w_mq, w_mo, g_ffn, w_gate, w_up, w_down)
    y_prompt = rmsnorm(hp, g_final)
    hs = x_sample
    for _ in range(DEPTH):
        hs, (s_attn_k, s_attn_v, s_conv, s_C, s_n, s_m) = decoder_layer(
            hs, cache_mem_k, cache_mem_v, state_conv, state_C, state_n, state_m, cache_attn_k, cache_attn_v,
            g_mix, w_in, conv_w, conv_b, b_gates, g_mh, w_out, g_xattn, w_mq, w_mo, g_ffn, w_gate, w_up, w_down)
    y_sample = rmsnorm(hs, g_final)
    return (y_prompt, y_sample, p_attn_k, p_attn_v, p_conv, p_C, p_n, p_m, p_mem_k, p_mem_v,
            s_attn_k, s_attn_v, s_conv, s_C, s_n, s_m)
```

```python
import functools

import jax
import jax.numpy as jnp
from jax import lax
from jax.experimental import pallas as pl
from jax.experimental.pallas import tpu as pltpu

F32 = jnp.float32
BF16 = jnp.bfloat16

EPS = 1e-6
MASKED = -1e30

MLSTM_HEADS = 4
MLSTM_DH = 128
MLSTM_W = MLSTM_HEADS * MLSTM_DH
ATTN_HEADS = 8
ATTN_DH = 64
ATTN_W = ATTN_HEADS * ATTN_DH
DILATED = ((128, 1), (512, 4), (2048, 16))
REL = 128
CONV_W = 4
MEM_HEADS = 4
MEM_DH = 256
LANES = 128
MIB = 1024 * 1024

COL_QK = 0
COL_VM = 2 * MLSTM_W
COL_OM = 3 * MLSTM_W
COL_QA = 4 * MLSTM_W
COL_KA = COL_QA + ATTN_W
COL_VA = COL_KA + ATTN_W
N_MAIN = COL_VA + ATTN_W


def _cparams(semantics, vmem_mib):
    return pltpu.CompilerParams(dimension_semantics=semantics, vmem_limit_bytes=vmem_mib * MIB)


def _rms(x, g):
    return x * lax.rsqrt(jnp.mean(x * x, axis=-1, keepdims=True) + EPS) * g


def _log_sigmoid(x):
    return jnp.minimum(x, 0.0) - jnp.log1p(jnp.exp(-jnp.abs(x)))


def _dot(a, b):
    return jnp.dot(a, b, preferred_element_type=F32)


def _dot_nt(a, b):
    return lax.dot_general(a, b, (((1,), (1,)), ((), ())), preferred_element_type=F32)


def _dot_tn(a, b):
    return lax.dot_general(a, b, (((0,), (0,)), ((), ())), preferred_element_type=F32)


def _norm_matmul_body(x_ref, g_ref, w_ref, *rest, with_side):
    if with_side:
        ws_ref, z_ref, side_ref, h_ref = rest
    else:
        z_ref, h_ref = rest

    @pl.when(pl.program_id(1) == 0)
    def _():
        h = _rms(x_ref[...], g_ref[...]).astype(BF16)
        h_ref[...] = h
        if with_side:
            side_ref[...] = _dot(h, ws_ref[...])

    z_ref[...] = _dot(h_ref[...], w_ref[...])


def _norm_matmul(x, g, w, w_side=None, *, tm, tn):
    n, d = x.shape
    c = w.shape[1]
    with_side = w_side is not None
    in_specs = [
        pl.BlockSpec((tm, d), lambda i, j: (i, 0)),
        pl.BlockSpec((1, d), lambda i, j: (0, 0)),
        pl.BlockSpec((d, tn), lambda i, j: (0, j)),
    ]
    out_specs = [pl.BlockSpec((tm, tn), lambda i, j: (i, j))]
    out_shape = [jax.ShapeDtypeStruct((n, c), F32)]
    args = [x, g.reshape(1, d), w]
    if with_side:
        in_specs.append(pl.BlockSpec((d, LANES), lambda i, j: (0, 0)))
        out_specs.append(pl.BlockSpec((tm, LANES), lambda i, j: (i, 0)))
        out_shape.append(jax.ShapeDtypeStruct((n, LANES), F32))
        args.append(w_side)
    vmem = 2 * tm * d * 4 + tm * d * 2 + 2 * d * tn * 2 + 2 * tm * tn * 4 + 4 * tm * LANES * 4
    res = pl.pallas_call(
        functools.partial(_norm_matmul_body, with_side=with_side),
        grid=(n // tm, c // tn),
        in_specs=in_specs,
        out_specs=out_specs,
        out_shape=out_shape,
        scratch_shapes=[pltpu.VMEM((tm, d), BF16)],
        compiler_params=_cparams(("arbitrary", "arbitrary"), vmem // MIB + 8),
        name="norm_matmul",
    )(*args)
    return res if with_side else res[0]


def _mlstm_prompt_body(qk_ref, v_ref, om_ref, gt_ref, cw_ref, cb_ref, bg_ref, gmh_ref,
                       hm_ref, conv_ref, c_out, n_out, m_out,
                       ubuf, c_s, n_s, m_s, *, chunk):
    L = chunk
    c = pl.program_id(1)
    last = pl.num_programs(1) - 1
    W2 = 2 * MLSTM_W

    @pl.when(c == 0)
    def _():
        ubuf[0:8, :] = jnp.zeros((8, W2), F32)
        c_s[...] = jnp.zeros_like(c_s)
        n_s[...] = jnp.zeros_like(n_s)
        m_s[...] = jnp.zeros_like(m_s)

    @pl.when(c > 0)
    def _():
        ubuf[0:8, :] = ubuf[L:L + 8, :]

    ubuf[8:L + 8, :] = qk_ref[...]
    y = jnp.broadcast_to(cb_ref[...], (L, W2))
    for j in range(CONV_W):
        y = y + ubuf[8 - (CONV_W - 1) + j:8 - (CONV_W - 1) + j + L, :] * cw_ref[j:j + 1, :]
    qk = y * jax.nn.sigmoid(y)

    gates = gt_ref[...] + bg_ref[...]
    col = lax.broadcasted_iota(jnp.int32, (L, LANES), 1)
    gates = jnp.where(col >= MLSTM_HEADS, _log_sigmoid(gates), gates)
    gates_t = gates.T

    t_idx = lax.broadcasted_iota(jnp.int32, (L, L), 0)
    s_idx = lax.broadcasted_iota(jnp.int32, (L, L), 1)
    causal = s_idx <= t_idx

    for h in range(MLSTM_HEADS):
        sl = slice(h * MLSTM_DH, (h + 1) * MLSTM_DH)
        qh = qk[:, sl]
        kh = qk[:, MLSTM_W + h * MLSTM_DH:MLSTM_W + (h + 1) * MLSTM_DH] * (MLSTM_DH ** -0.5)
        vh = v_ref[:, sl]
        i_col = gates[:, h:h + 1]
        f_col = gates[:, MLSTM_HEADS + h:MLSTM_HEADS + h + 1]
        i_row = gates_t[h:h + 1, :]
        f_row = gates_t[MLSTM_HEADS + h:MLSTM_HEADS + h + 1, :]
        m_prev = m_s[h:h + 1, 0:1]
        c_prev = c_s[h]
        n_prev = n_s[h:h + 1, :]

        b_col = jnp.sum(jnp.where(causal, f_row, 0.0), axis=1, keepdims=True)
        b_row = jnp.sum(jnp.where(t_idx <= s_idx, f_col, 0.0), axis=0, keepdims=True)
        dmat = jnp.where(causal, b_col - b_row + i_row, MASKED)
        a_col = b_col + m_prev
        mt = jnp.maximum(a_col, jnp.max(dmat, axis=1, keepdims=True))
        wa = jnp.exp(a_col - mt)
        qb = qh.astype(BF16)
        kb = kh.astype(BF16)
        vb = vh.astype(BF16)
        sc = _dot_nt(qb, kb) * jnp.exp(dmat - mt)
        num = wa * _dot(qb, c_prev.astype(BF16)) + _dot(sc.astype(BF16), vb)
        den = wa * jnp.sum(qh * n_prev, axis=1, keepdims=True) + jnp.sum(sc, axis=1, keepdims=True)
        hh = num / jnp.maximum(jnp.abs(den), jnp.exp(-mt))

        bl = b_row[:, L - 1:L]
        g_col = bl - b_col + i_col
        m_new = jnp.maximum(bl + m_prev, jnp.max(g_col, axis=0, keepdims=True))
        wc = jnp.exp(bl + m_prev - m_new)
        kw = jnp.exp(g_col - m_new) * kh
        c_s[h] = wc * c_prev + _dot_tn(kw.astype(BF16), vb)
        n_s[h:h + 1, :] = wc * n_prev + jnp.sum(kw, axis=0, keepdims=True)
        m_s[h:h + 1, :] = jnp.broadcast_to(m_new, (1, LANES))

        mu = jnp.mean(hh, axis=1, keepdims=True)
        var = jnp.mean(jnp.square(hh - mu), axis=1, keepdims=True)
        hn = (hh - mu) * lax.rsqrt(var + EPS) * gmh_ref[:, sl]
        hm_ref[:, sl] = hn * jax.nn.sigmoid(om_ref[:, sl])

    @pl.when(c == last)
    def _():
        conv_ref[0] = ubuf[L + 8 - (CONV_W - 1):L + 8, :]
        c_out[0] = c_s[...]
        n_out[0] = n_s[...]
        m_out[0] = m_s[...]


def _mlstm_prompt(z, gates, conv_w, conv_b, bg_row, g_mh, *, batch, seq, chunk):
    nc = seq // chunk
    n = batch * seq
    row = lambda b, c: b * nc + c
    W2 = 2 * MLSTM_W
    full = lambda shape: pl.BlockSpec(shape, lambda b, c: (0,) * len(shape))
    return pl.pallas_call(
        functools.partial(_mlstm_prompt_body, chunk=chunk),
        grid=(batch, nc),
        in_specs=[
            pl.BlockSpec((chunk, W2), lambda b, c: (row(b, c), COL_QK // W2)),
            pl.BlockSpec((chunk, MLSTM_W), lambda b, c: (row(b, c), COL_VM // MLSTM_W)),
            pl.BlockSpec((chunk, MLSTM_W), lambda b, c: (row(b, c), COL_OM // MLSTM_W)),
            pl.BlockSpec((chunk, LANES), lambda b, c: (row(b, c), 0)),
            full((CONV_W, W2)), full((1, W2)), full((1, LANES)), full((1, MLSTM_W)),
        ],
        out_specs=[
            pl.BlockSpec((chunk, MLSTM_W), lambda b, c: (row(b, c), 0)),
            pl.BlockSpec((1, CONV_W - 1, W2), lambda b, c: (b, 0, 0)),
            pl.BlockSpec((1, MLSTM_HEADS, MLSTM_DH, MLSTM_DH), lambda b, c: (b, 0, 0, 0)),
            pl.BlockSpec((1, MLSTM_HEADS, MLSTM_DH), lambda b, c: (b, 0, 0)),
            pl.BlockSpec((1, MLSTM_HEADS, LANES), lambda b, c: (b, 0, 0)),
        ],
        out_shape=[
            jax.ShapeDtypeStruct((n, MLSTM_W), F32),
            jax.ShapeDtypeStruct((batch, CONV_W - 1, W2), F32),
            jax.ShapeDtypeStruct((batch, MLSTM_HEADS, MLSTM_DH, MLSTM_DH), F32),
            jax.ShapeDtypeStruct((batch, MLSTM_HEADS, MLSTM_DH), F32),
            jax.ShapeDtypeStruct((batch, MLSTM_HEADS, LANES), F32),
        ],
        scratch_shapes=[
            pltpu.VMEM((chunk + 8, W2), F32),
            pltpu.VMEM((MLSTM_HEADS, MLSTM_DH, MLSTM_DH), F32),
            pltpu.VMEM((MLSTM_HEADS, MLSTM_DH), F32),
            pltpu.VMEM((MLSTM_HEADS, LANES), F32),
        ],
        compiler_params=_cparams(("arbitrary", "arbitrary"), 32),
        name="mlstm_prompt",
    )(z, z, z, gates, conv_w, conv_b, bg_row, g_mh)


def _dilated_prompt_body(q_ref, k_ref, v_ref, o_ref, kk, vv, acc, mm, ll, *, tq):
    i = pl.program_id(2)
    hd = ATTN_DH

    @pl.when(i == 0)
    def _():
        kk[0:tq, :] = jnp.zeros((tq, LANES), F32)
        vv[0:tq, :] = jnp.zeros((tq, LANES), F32)

    @pl.when(i > 0)
    def _():
        kk[0:tq, :] = kk[tq:2 * tq, :]
        vv[0:tq, :] = vv[tq:2 * tq, :]

    kk[tq:2 * tq, :] = k_ref[...]
    vv[tq:2 * tq, :] = v_ref[...]

    lane = lax.broadcasted_iota(jnp.int32, (REL, LANES), 1)
    first_head = lane < hd
    a_idx = lax.broadcasted_iota(jnp.int32, (REL, 2 * REL), 0)
    j_idx = lax.broadcasted_iota(jnp.int32, (REL, 2 * REL), 1)
    band = (j_idx >= a_idx) & (j_idx <= a_idx + REL)

    for bi, (window, d) in enumerate(DILATED):
        assert window // d == REL and tq % (REL * d) == 0

        def block(it, carry, d=d, bi=bi):
            r = lax.rem(it, d)
            cb = lax.div(it, d)
            q_start = r + d * REL * cb
            k_start = tq + r + d * REL * (cb - 1)
            rows = pl.ds(q_start, REL, stride=d)
            krows = pl.ds(k_start, 2 * REL, stride=d)
            q2 = q_ref[rows, :] * (hd ** -0.5)
            k2 = kk[krows, :].astype(BF16)
            v2 = vv[krows, :].astype(BF16)
            lo = jnp.where(jnp.logical_and(i == 0, cb == 0), REL, 0)
            valid = band & (j_idx >= lo)
            parts = []
            for hh in range(2):
                sel = first_head if hh == 0 else jnp.logical_not(first_head)
                qm = jnp.where(sel, q2, 0.0).astype(BF16)
                s = jnp.where(valid, _dot_nt(qm, k2), MASKED)
                mx = jnp.max(s, axis=1, keepdims=True)
                p = jnp.exp(s - mx)
                parts.append((mx, jnp.sum(p, axis=1, keepdims=True), _dot(p.astype(BF16), v2)))
            m_blk = jnp.where(first_head, parts[0][0], parts[1][0])
            l_blk = jnp.where(first_head, parts[0][1], parts[1][1])
            o_blk = jnp.where(first_head, parts[0][2], parts[1][2])
            if bi == 0:
                mm[rows, :] = m_blk
                ll[rows, :] = l_blk
                acc[rows, :] = o_blk
            else:
                m_old = mm[rows, :]
                m_new = jnp.maximum(m_old, m_blk)
                w_old = jnp.exp(m_old - m_new)
                w_blk = jnp.exp(m_blk - m_new)
                mm[rows, :] = m_new
                ll[rows, :] = ll[rows, :] * w_old + l_blk * w_blk
                acc[rows, :] = acc[rows, :] * w_old + o_blk * w_blk
            return carry

        lax.fori_loop(0, tq // REL, block, 0)

    o_ref[...] = acc[...] / ll[...]


def _dilated_prompt(z, *, batch, seq, tq):
    nt = seq // tq
    n = batch * seq
    pairs = ATTN_W // LANES

    def spec(col0):
        return pl.BlockSpec((tq, LANES), lambda b, p, i: (b * nt + i, col0 // LANES + p))

    return pl.pallas_call(
        functools.partial(_dilated_prompt_body, tq=tq),
        grid=(batch, pairs, nt),
        in_specs=[spec(COL_QA), spec(COL_KA), spec(COL_VA)],
        out_specs=pl.BlockSpec((tq, LANES), lambda b, p, i: (b * nt + i, p)),
        out_shape=jax.ShapeDtypeStruct((n, ATTN_W), F32),
        scratch_shapes=[
            pltpu.VMEM((2 * tq, LANES), F32), pltpu.VMEM((2 * tq, LANES), F32),
            pltpu.VMEM((tq, LANES), F32), pltpu.VMEM((tq, LANES), F32), pltpu.VMEM((tq, LANES), F32),
        ],
        compiler_params=_cparams(("arbitrary", "arbitrary", "arbitrary"), 32),
        name="dilated_prompt",
    )(z, z, z)


def _outproj_q_body(x_ref, hm_ref, ha_ref, wom_ref, woa_ref, gx_ref, wq_ref, x1_ref, q_ref):
    x1 = x_ref[...] + _dot(hm_ref[...].astype(BF16), wom_ref[...]) + _dot(ha_ref[...].astype(BF16), woa_ref[...])
    x1_ref[...] = x1
    q_ref[...] = _dot(_rms(x1, gx_ref[...]).astype(BF16), wq_ref[...]).astype(BF16)


def _outproj_q(x, hm, ha, w_out_m, w_out_a, g_x, w_mq, *, tm):
    n, d = x.shape
    row = lambda w: pl.BlockSpec((tm, w), lambda i: (i, 0))
    full = lambda shape: pl.BlockSpec(shape, lambda i: (0, 0))
    return pl.pallas_call(
        _outproj_q_body,
        grid=(n // tm,),
        in_specs=[row(d), row(MLSTM_W), row(ATTN_W), full((MLSTM_W, d)), full((ATTN_W, d)),
                  full((1, d)), full((d, d))],
        out_specs=[row(d), row(d)],
        out_shape=[jax.ShapeDtypeStruct((n, d), F32), jax.ShapeDtypeStruct((n, d), BF16)],
        compiler_params=_cparams(("arbitrary",), 40),
        name="outproj_q",
    )(x, hm, ha, w_out_m, w_out_a, g_x.reshape(1, d), w_mq)


def _xattn_prompt_body(q_ref, k_ref, v_ref, o_ref):
    for h in range(MEM_HEADS):
        sl = slice(h * MEM_DH, (h + 1) * MEM_DH)
        s = _dot_nt(q_ref[:, sl], k_ref[:, sl].astype(BF16)) * (MEM_DH ** -0.5)
        e = jnp.exp(s - jnp.max(s, axis=1, keepdims=True))
        p = e / jnp.sum(e, axis=1, keepdims=True)
        o_ref[:, sl] = _dot(p.astype(BF16), v_ref[:, sl].astype(BF16)).astype(BF16)


def _xattn_prompt(q, zmem, *, batch, seq, mem_len, tm):
    n, d = q.shape
    nt = seq // tm
    return pl.pallas_call(
        _xattn_prompt_body,
        grid=(batch, nt),
        in_specs=[
            pl.BlockSpec((tm, d), lambda b, i: (b * nt + i, 0)),
            pl.BlockSpec((mem_len, d), lambda b, i: (b, 0)),
            pl.BlockSpec((mem_len, d), lambda b, i: (b, 1)),
        ],
        out_specs=pl.BlockSpec((tm, d), lambda b, i: (b * nt + i, 0)),
        out_shape=jax.ShapeDtypeStruct((n, d), BF16),
        compiler_params=_cparams(("arbitrary", "arbitrary"), 32),
        name="xattn_prompt",
    )(q, zmem, zmem)


def _post_ffn_body(x1_ref, o_ref, wmo_ref, gf_ref, wg_ref, wu_ref, wd_ref, gfin_ref, y_ref, h_s, acc_s):
    j = pl.program_id(1)

    @pl.when(j == 0)
    def _():
        x2 = x1_ref[...] + _dot(o_ref[...], wmo_ref[...])
        acc_s[...] = x2
        h_s[...] = _rms(x2, gf_ref[...]).astype(BF16)

    h = h_s[...]
    g = _dot(h, wg_ref[...])
    u = _dot(h, wu_ref[...])
    acc_s[...] += _dot((g * jax.nn.sigmoid(g) * u).astype(BF16), wd_ref[...])

    @pl.when(j == pl.num_programs(1) - 1)
    def _():
        y_ref[...] = _rms(acc_s[...], gfin_ref[...])


def _post_ffn(x1, o, w_mo, g_ffn, w_gate, w_up, w_down, g_final, *, tm, tf):
    n, d = x1.shape
    dff = w_gate.shape[1]
    row = lambda: pl.BlockSpec((tm, d), lambda i, j: (i, 0))
    full = lambda shape: pl.BlockSpec(shape, lambda i, j: (0, 0))
    return pl.pallas_call(
        _post_ffn_body,
        grid=(n // tm, dff // tf),
        in_specs=[row(), row(), full((d, d)), full((1, d)),
                  pl.BlockSpec((d, tf), lambda i, j: (0, j)), pl.BlockSpec((d, tf), lambda i, j: (0, j)),
                  pl.BlockSpec((tf, d), lambda i, j: (j, 0)), full((1, d))],
        out_specs=row(),
        out_shape=jax.ShapeDtypeStruct((n, d), F32),
        scratch_shapes=[pltpu.VMEM((tm, d), BF16), pltpu.VMEM((tm, d), F32)],
        compiler_params=_cparams(("arbitrary", "arbitrary"), 48),
        name="post_ffn",
    )(x1, o, w_mo, g_ffn.reshape(1, d), w_gate, w_up, w_down, g_final.reshape(1, d))


def _mlstm_sample_body(qk_ref, v_ref, om_ref, gt_ref, conv_ref, c_ref, n_ref, m_ref,
                       cw_ref, cb_ref, bg_ref, gmh_ref,
                       hm_ref, conv_out, c_out, n_out, m_out, *, group, steps):
    T = steps
    W2 = 2 * MLSTM_W
    t_idx = lax.broadcasted_iota(jnp.int32, (T, T), 0)
    s_idx = lax.broadcasted_iota(jnp.int32, (T, T), 1)
    causal = s_idx <= t_idx
    eye = s_idx == t_idx
    col = lax.broadcasted_iota(jnp.int32, (T, LANES), 1)
    pad = jnp.zeros((8 - T, MLSTM_DH), F32)

    def to_row(v_col):
        return jnp.sum(jnp.where(eye, v_col, 0.0), axis=0, keepdims=True)

    def one(b, carry):
        up = jnp.concatenate([conv_ref[b], qk_ref[b]], axis=0)
        y = jnp.broadcast_to(cb_ref[...], (T, W2))
        for j in range(CONV_W):
            y = y + up[j:j + T, :] * cw_ref[j:j + 1, :]
        conv_out[b] = up[T:T + CONV_W - 1, :]
        qk = y * jax.nn.sigmoid(y)
        gates = gt_ref[b] + bg_ref[...]
        gates = jnp.where(col >= MLSTM_HEADS, _log_sigmoid(gates), gates)
        v_all = v_ref[b]
        om_all = om_ref[b]
        m_all = m_ref[b]
        hs = []
        for h in range(MLSTM_HEADS):
            sl = slice(h * MLSTM_DH, (h + 1) * MLSTM_DH)
            qh = qk[:, sl]
            kh = qk[:, MLSTM_W + h * MLSTM_DH:MLSTM_W + (h + 1) * MLSTM_DH] * (MLSTM_DH ** -0.5)
            vh = v_all[:, sl]
            i_col = gates[:, h:h + 1]
            f_col = gates[:, MLSTM_HEADS + h:MLSTM_HEADS + h + 1]
            i_row = to_row(i_col)
            f_row = to_row(f_col)
            m_prev = m_all[:, h:h + 1]
            c_prev = c_ref[b, h]
            n_prev = n_ref[b, h:h + 1, :]

            b_col = jnp.sum(jnp.where(causal, f_row, 0.0), axis=1, keepdims=True)
            b_row = to_row(b_col)
            dmat = jnp.where(causal, b_col - b_row + i_row, MASKED)
            a_col = b_col + m_prev
            mt = jnp.maximum(a_col, jnp.max(dmat, axis=1, keepdims=True))
            wa = jnp.exp(a_col - mt)
            decay = jnp.exp(dmat - mt)
            qkt = jnp.zeros((T, T), F32)
            for s in range(T):
                dots = jnp.sum(qh * kh[s:s + 1, :], axis=1, keepdims=True)
                qkt = jnp.where(s_idx == s, dots, qkt)
            sc = qkt * decay
            q8 = jnp.concatenate([qh, pad], axis=0).astype(BF16)
            inter = _dot(q8, c_prev.astype(BF16))[0:T, :]
            intra = jnp.zeros((T, MLSTM_DH), F32)
            for s in range(T):
                intra = intra + sc[:, s:s + 1] * vh[s:s + 1, :]
            num = wa * inter + intra
            den = wa * jnp.sum(qh * n_prev, axis=1, keepdims=True) + jnp.sum(sc, axis=1, keepdims=True)
            hh = num / jnp.maximum(jnp.abs(den), jnp.exp(-mt))

            bl = b_col[T - 1:T, :]
            g_col = bl - b_col + i_col
            m_new = jnp.maximum(bl + m_prev, jnp.max(g_col, axis=0, keepdims=True))
            wc = jnp.exp(bl + m_prev - m_new)
            kw = jnp.exp(g_col - m_new) * kh
            kw8 = jnp.concatenate([kw, pad], axis=0).astype(BF16)
            v8 = jnp.concatenate([vh, pad], axis=0).astype(BF16)
            c_out[b, h] = wc * c_prev + _dot_tn(kw8, v8)
            n_out[b, h:h + 1, :] = wc * n_prev + jnp.sum(kw, axis=0, keepdims=True)
            m_out[b, h:h + 1, :] = jnp.broadcast_to(m_new, (1, LANES))

            mu = jnp.mean(hh, axis=1, keepdims=True)
            var = jnp.mean(jnp.square(hh - mu), axis=1, keepdims=True)
            hn = (hh - mu) * lax.rsqrt(var + EPS) * gmh_ref[:, sl]
            hs.append(hn * jax.nn.sigmoid(om_all[:, sl]))
        hm_ref[b] = jnp.concatenate(hs, axis=1)
        return carry

    lax.fori_loop(0, group, one, 0)


def _mlstm_sample(z3, gates3, conv, c0, n0, m0, conv_w, conv_b, bg_row, g_mh, *, group):
    batch, steps, _ = z3.shape
    W2 = 2 * MLSTM_W
    full = lambda shape: pl.BlockSpec(shape, lambda g: (0,) * len(shape))
    return pl.pallas_call(
        functools.partial(_mlstm_sample_body, group=group, steps=steps),
        grid=(batch // group,),
        in_specs=[
            pl.BlockSpec((group, steps, W2), lambda g: (g, 0, COL_QK // W2)),
            pl.BlockSpec((group, steps, MLSTM_W), lambda g: (g, 0, COL_VM // MLSTM_W)),
            pl.BlockSpec((group, steps, MLSTM_W), lambda g: (g, 0, COL_OM // MLSTM_W)),
            pl.BlockSpec((group, steps, LANES), lambda g: (g, 0, 0)),
            pl.BlockSpec((group, CONV_W - 1, W2), lambda g: (g, 0, 0)),
            pl.BlockSpec((group, MLSTM_HEADS, MLSTM_DH, MLSTM_DH), lambda g: (g, 0, 0, 0)),
            pl.BlockSpec((group, MLSTM_HEADS, MLSTM_DH), lambda g: (g, 0, 0)),
            pl.BlockSpec((group, 1, LANES), lambda g: (g, 0, 0)),
            full((CONV_W, W2)), full((1, W2)), full((1, LANES)), full((1, MLSTM_W)),
        ],
        out_specs=[
            pl.BlockSpec((group, steps, MLSTM_W), lambda g: (g, 0, 0)),
            pl.BlockSpec((group, CONV_W - 1, W2), lambda g: (g, 0, 0)),
            pl.BlockSpec((group, MLSTM_HEADS, MLSTM_DH, MLSTM_DH), lambda g: (g, 0, 0, 0)),
            pl.BlockSpec((group, MLSTM_HEADS, MLSTM_DH), lambda g: (g, 0, 0)),
            pl.BlockSpec((group, MLSTM_HEADS, LANES), lambda g: (g, 0, 0)),
        ],
        out_shape=[
            jax.ShapeDtypeStruct((batch, steps, MLSTM_W), F32),
            jax.ShapeDtypeStruct((batch, CONV_W - 1, W2), F32),
            jax.ShapeDtypeStruct((batch, MLSTM_HEADS, MLSTM_DH, MLSTM_DH), F32),
            jax.ShapeDtypeStruct((batch, MLSTM_HEADS, MLSTM_DH), F32),
            jax.ShapeDtypeStruct((batch, MLSTM_HEADS, LANES), F32),
        ],
        compiler_params=_cparams(("arbitrary",), 32),
        name="mlstm_sample",
    )(z3, z3, z3, gates3, conv, c0, n0, m0, conv_w, conv_b, bg_row, g_mh)


def _branch_count(dist):
    cnt = jnp.zeros(dist.shape, F32)
    for window, d in DILATED:
        ok = (dist >= 0) & (dist <= window) & (lax.rem(jnp.maximum(dist, 0), d) == 0)
        cnt = cnt + jnp.where(ok, 1.0, 0.0)
    return cnt


def _dilated_sample_body(q_ref, kn_ref, vn_ref, kc_ref, vc_ref, o_ref, *, steps, wb):
    Q = q_ref.shape[1]
    t_c = lax.broadcasted_iota(jnp.int32, (Q, wb), 0)
    pos_c = lax.broadcasted_iota(jnp.int32, (Q, wb), 1)
    cnt_c = _branch_count(wb + t_c - pos_c)
    t_n = lax.broadcasted_iota(jnp.int32, (Q, Q), 0)
    s_n = lax.broadcasted_iota(jnp.int32, (Q, Q), 1)
    cnt_n = jnp.where(s_n < steps, _branch_count(t_n - s_n), 0.0)
    q_all = q_ref[0] * (ATTN_DH ** -0.5)
    kn_all = kn_ref[0]
    vn_all = vn_ref[0]
    outs = []
    for h in range(ATTN_HEADS):
        sl = slice(h * ATTN_DH, (h + 1) * ATTN_DH)
        qh = q_all[:, sl]
        knh = kn_all[:, sl]
        vnh = vn_all[:, sl]
        s_c = _dot(qh.astype(BF16), kc_ref[0, h].astype(BF16))
        s_new = jnp.zeros((Q, Q), F32)
        for s in range(steps):
            dots = jnp.sum(qh * knh[s:s + 1, :], axis=1, keepdims=True)
            s_new = jnp.where(s_n == s, dots, s_new)
        s_c = jnp.where(cnt_c > 0, s_c, MASKED)
        s_new = jnp.where(cnt_n > 0, s_new, MASKED)
        mx = jnp.maximum(jnp.max(s_c, axis=1, keepdims=True), jnp.max(s_new, axis=1, keepdims=True))
        p_c = cnt_c * jnp.exp(s_c - mx)
        p_n = cnt_n * jnp.exp(s_new - mx)
        den = jnp.sum(p_c, axis=1, keepdims=True) + jnp.sum(p_n, axis=1, keepdims=True)
        o = _dot_nt(p_c.astype(BF16), vc_ref[0, h].astype(BF16))
        for s in range(steps):
            o = o + p_n[:, s:s + 1] * vnh[s:s + 1, :]
        outs.append(o / den)
    o_ref[0] = jnp.concatenate(outs, axis=1)


def _dilated_sample(q8, kn8, vn8, kc_t, vc_t, *, steps):
    batch, qrows, _ = q8.shape
    wb = kc_t.shape[-1]
    new = lambda: pl.BlockSpec((1, qrows, ATTN_W), lambda b: (b, 0, 0))
    cache = lambda: pl.BlockSpec((1, ATTN_HEADS, ATTN_DH, wb), lambda b: (b, 0, 0, 0))
    return pl.pallas_call(
        functools.partial(_dilated_sample_body, steps=steps, wb=wb),
        grid=(batch,),
        in_specs=[new(), new(), new(), cache(), cache()],
        out_specs=new(),
        out_shape=jax.ShapeDtypeStruct((batch, qrows, ATTN_W), F32),
        compiler_params=_cparams(("arbitrary",), 40),
        name="dilated_sample",
    )(q8, kn8, vn8, kc_t, vc_t)


def _xattn_sample_body(q_ref, k_ref, v_ref, o_ref):
    Q = q_ref.shape[1]
    mem_len = k_ref.shape[1]
    rows = mem_len * MEM_HEADS
    k2 = k_ref[0].reshape(rows, MEM_DH).astype(BF16)
    v2 = v_ref[0].reshape(rows, MEM_DH).astype(BF16)
    q_all = q_ref[0]
    qs = jnp.concatenate([q_all[:, h * MEM_DH:(h + 1) * MEM_DH] for h in range(MEM_HEADS)], axis=0)
    s = _dot_nt(qs, k2) * (MEM_DH ** -0.5)
    q_head = lax.div(lax.broadcasted_iota(jnp.int32, s.shape, 0), Q)
    k_head = lax.rem(lax.broadcasted_iota(jnp.int32, s.shape, 1), MEM_HEADS)
    s = jnp.where(q_head == k_head, s, MASKED)
    e = jnp.exp(s - jnp.max(s, axis=1, keepdims=True))
    p = e / jnp.sum(e, axis=1, keepdims=True)
    o = _dot(p.astype(BF16), v2)
    for h in range(MEM_HEADS):
        o_ref[0, :, h * MEM_DH:(h + 1) * MEM_DH] = o[h * Q:(h + 1) * Q, :].astype(BF16)


def _xattn_sample(q8, mem_k, mem_v):
    batch, qrows, d = q8.shape
    mem_len = mem_k.shape[1]
    mem = lambda: pl.BlockSpec((1, mem_len, MEM_HEADS, MEM_DH), lambda b: (b, 0, 0, 0))
    return pl.pallas_call(
        _xattn_sample_body,
        grid=(batch,),
        in_specs=[pl.BlockSpec((1, qrows, d), lambda b: (b, 0, 0)), mem(), mem()],
        out_specs=pl.BlockSpec((1, qrows, d), lambda b: (b, 0, 0)),
        out_shape=jax.ShapeDtypeStruct((batch, qrows, d), BF16),
        compiler_params=_cparams(("arbitrary",), 32),
        name="xattn_sample",
    )(q8, mem_k, mem_v)


def _pad_rows(a, rows):
    return jnp.pad(a, ((0, 0), (0, rows - a.shape[1]), (0, 0)))


def kernel(x_prompt, x_sample, mem_prompt, cache_attn_k, cache_attn_v, cache_mem_k, cache_mem_v, state_conv, state_C, state_n, state_m, g_mix, w_in, conv_w, conv_b, b_gates, g_mh, w_out, g_mem, w_mk, w_mv, g_xattn, w_mq, w_mo, g_ffn, w_gate, w_up, w_down, g_final):
    B, T, D = x_prompt.shape
    SB, ST, _ = x_sample.shape
    mem_len = mem_prompt.shape[1]
    W2 = 2 * MLSTM_W
    gate0 = COL_QA
    n_gate = 2 * MLSTM_HEADS

    w_main = jnp.concatenate([w_in[:, :gate0], w_in[:, gate0 + n_gate:]], axis=1).astype(BF16)
    w_gcols = jnp.pad(w_in[:, gate0:gate0 + n_gate], ((0, 0), (0, LANES - n_gate))).astype(BF16)
    bg_row = jnp.pad(b_gates, (0, LANES - n_gate)).reshape(1, LANES)
    w_mem = jnp.concatenate([w_mk, w_mv], axis=1).astype(BF16)
    w_out_m = w_out[:MLSTM_W].astype(BF16)
    w_out_a = w_out[MLSTM_W:].astype(BF16)
    w_mq_b, w_mo_b = w_mq.astype(BF16), w_mo.astype(BF16)
    w_gate_b, w_up_b, w_down_b = w_gate.astype(BF16), w_up.astype(BF16), w_down.astype(BF16)
    conv_b2 = conv_b.reshape(1, W2)
    g_mh2 = g_mh.reshape(1, MLSTM_W)

    xp = x_prompt.reshape(B * T, D)
    zmem = _norm_matmul(mem_prompt.reshape(B * mem_len, D), g_mem, w_mem, tm=B * mem_len, tn=512)
    z, gates = _norm_matmul(xp, g_mix, w_main, w_gcols, tm=1024, tn=512)
    hm, p_conv, p_C, p_n, p_m = _mlstm_prompt(z, gates, conv_w, conv_b2, bg_row, g_mh2,
                                              batch=B, seq=T, chunk=128)
    ha = _dilated_prompt(z, batch=B, seq=T, tq=2048)
    x1, q = _outproj_q(xp, hm, ha, w_out_m, w_out_a, g_xattn, w_mq_b, tm=512)
    o = _xattn_prompt(q, zmem, batch=B, seq=T, mem_len=mem_len, tm=512)
    y_prompt = _post_ffn(x1, o, w_mo_b, g_ffn, w_gate_b, w_up_b, w_down_b, g_final, tm=1024, tf=256)

    keep = min(DILATED[-1][0], T)
    z3p = z.reshape(B, T, N_MAIN)
    p_attn_k = z3p[:, T - keep:, COL_KA:COL_KA + ATTN_W].reshape(B, keep, ATTN_HEADS, ATTN_DH)
    p_attn_v = z3p[:, T - keep:, COL_VA:COL_VA + ATTN_W].reshape(B, keep, ATTN_HEADS, ATTN_DH)
    p_mem_k = zmem[:, :D].reshape(B, mem_len, MEM_HEADS, MEM_DH)
    p_mem_v = zmem[:, D:].reshape(B, mem_len, MEM_HEADS, MEM_DH)

    xs = x_sample.reshape(SB * ST, D)
    zs, gates_s = _norm_matmul(xs, g_mix, w_main, w_gcols, tm=SB * ST, tn=512)
    zs3 = zs.reshape(SB, ST, N_MAIN)
    m0 = jnp.pad(state_m, ((0, 0), (0, LANES - MLSTM_HEADS))).reshape(SB, 1, LANES)
    hm_s, s_conv, s_C, s_n, s_m = _mlstm_sample(
        zs3, gates_s.reshape(SB, ST, LANES), state_conv, state_C, state_n, m0,
        conv_w, conv_b2, bg_row, g_mh2, group=8)
    qrows = 8
    qa8 = _pad_rows(zs3[:, :, COL_QA:COL_QA + ATTN_W], qrows)
    ka8 = _pad_rows(zs3[:, :, COL_KA:COL_KA + ATTN_W], qrows)
    va8 = _pad_rows(zs3[:, :, COL_VA:COL_VA + ATTN_W], qrows)
    ha_s = _dilated_sample(qa8, ka8, va8, jnp.transpose(cache_attn_k, (0, 2, 3, 1)),
                           jnp.transpose(cache_attn_v, (0, 2, 3, 1)), steps=ST)[:, :ST]
    x1s, qs = _outproj_q(xs, hm_s.reshape(SB * ST, MLSTM_W), ha_s.reshape(SB * ST, ATTN_W),
                         w_out_m, w_out_a, g_xattn, w_mq_b, tm=SB * ST)
    os_ = _xattn_sample(_pad_rows(qs.reshape(SB, ST, D), qrows), cache_mem_k, cache_mem_v)[:, :ST]
    y_sample = _post_ffn(x1s, os_.reshape(SB * ST, D), w_mo_b, g_ffn, w_gate_b, w_up_b, w_down_b,
                         g_final, tm=SB * ST, tf=256)

    s_attn_k = zs3[:, :, COL_KA:COL_KA + ATTN_W].reshape(SB, ST, ATTN_HEADS, ATTN_DH)
    s_attn_v = zs3[:, :, COL_VA:COL_VA + ATTN_W].reshape(SB, ST, ATTN_HEADS, ATTN_DH)

    return (y_prompt.reshape(B, T, D), y_sample.reshape(SB, ST, D), p_attn_k, p_attn_v,
            p_conv, p_C, p_n, p_m[:, :, 0], p_mem_k, p_mem_v,
            s_attn_k, s_attn_v, s_conv, s_C, s_n, s_m[:, :, 0])
```

```python
import functools

import jax
import jax.numpy as jnp
from jax import lax
from jax.experimental import pallas as pl
from jax.experimental.pallas import tpu as pltpu

F32 = jnp.float32
BF16 = jnp.bfloat16

EPS = 1e-6
MASKED = -1e30

MLSTM_HEADS = 4
MLSTM_DH = 128
MLSTM_W = MLSTM_HEADS * MLSTM_DH
ATTN_HEADS = 8
ATTN_DH = 64
ATTN_W = ATTN_HEADS * ATTN_DH
DILATED = ((128, 1), (512, 4), (2048, 16))
REL = 128
CONV_W = 4
MEM_HEADS = 4
MEM_DH = 256
LANES = 128
MIB = 1024 * 1024

COL_QK = 0
COL_VM = 2 * MLSTM_W
COL_OM = 3 * MLSTM_W
COL_QA = 4 * MLSTM_W
COL_KA = COL_QA + ATTN_W
COL_VA = COL_KA + ATTN_W
N_MAIN = COL_VA + ATTN_W


def _cparams(semantics, vmem_mib):
    return pltpu.CompilerParams(dimension_semantics=semantics, vmem_limit_bytes=vmem_mib * MIB)


def _rms(x, g):
    return x * lax.rsqrt(jnp.mean(x * x, axis=-1, keepdims=True) + EPS) * g


def _log_sigmoid(x):
    return jnp.minimum(x, 0.0) - jnp.log1p(jnp.exp(-jnp.abs(x)))


def _dot(a, b):
    return jnp.dot(a, b, preferred_element_type=F32)


def _dot_nt(a, b):
    return lax.dot_general(a, b, (((1,), (1,)), ((), ())), preferred_element_type=F32)


def _dot_tn(a, b):
    return lax.dot_general(a, b, (((0,), (0,)), ((), ())), preferred_element_type=F32)


def _norm_matmul_body(x_ref, g_ref, w_ref, *rest, with_side):
    if with_side:
        ws_ref, z_ref, side_ref, h_ref = rest
    else:
        z_ref, h_ref = rest

    @pl.when(pl.program_id(1) == 0)
    def _():
        h = _rms(x_ref[...], g_ref[...]).astype(BF16)
        h_ref[...] = h
        if with_side:
            side_ref[...] = _dot(h, ws_ref[...])

    z_ref[...] = _dot(h_ref[...], w_ref[...])


def _norm_matmul(x, g, w, w_side=None, *, tm, tn):
    n, d = x.shape
    c = w.shape[1]
    with_side = w_side is not None
    in_specs = [
        pl.BlockSpec((tm, d), lambda i, j: (i, 0)),
        pl.BlockSpec((1, d), lambda i, j: (0, 0)),
        pl.BlockSpec((d, tn), lambda i, j: (0, j)),
    ]
    out_specs = [pl.BlockSpec((tm, tn), lambda i, j: (i, j))]
    out_shape = [jax.ShapeDtypeStruct((n, c), F32)]
    args = [x, g.reshape(1, d), w]
    if with_side:
        in_specs.append(pl.BlockSpec((d, LANES), lambda i, j: (0, 0)))
        out_specs.append(pl.BlockSpec((tm, LANES), lambda i, j: (i, 0)))
        out_shape.append(jax.ShapeDtypeStruct((n, LANES), F32))
        args.append(w_side)
    vmem = 2 * tm * d * 4 + tm * d * 2 + 2 * d * tn * 2 + 2 * tm * tn * 4 + 4 * tm * LANES * 4
    res = pl.pallas_call(
        functools.partial(_norm_matmul_body, with_side=with_side),
        grid=(n // tm, c // tn),
        in_specs=in_specs,
        out_specs=out_specs,
        out_shape=out_shape,
        scratch_shapes=[pltpu.VMEM((tm, d), BF16)],
        compiler_params=_cparams(("arbitrary", "arbitrary"), vmem // MIB + 8),
        name="norm_matmul",
    )(*args)
    return res if with_side else res[0]


def _mlstm_prompt_body(qk_ref, v_ref, om_ref, gt_ref, cw_ref, cb_ref, bg_ref, gmh_ref,
                       hm_ref, conv_ref, c_out, n_out, m_out,
                       ubuf, c_s, n_s, m_s, *, chunk):
    L = chunk
    c = pl.program_id(1)
    last = pl.num_programs(1) - 1
    W2 = 2 * MLSTM_W

    @pl.when(c == 0)
    def _():
        ubuf[0:8, :] = jnp.zeros((8, W2), F32)
        c_s[...] = jnp.zeros_like(c_s)
        n_s[...] = jnp.zeros_like(n_s)
        m_s[...] = jnp.zeros_like(m_s)

    @pl.when(c > 0)
    def _():
        ubuf[0:8, :] = ubuf[L:L + 8, :]

    ubuf[8:L + 8, :] = qk_ref[...]
    y = jnp.broadcast_to(cb_ref[...], (L, W2))
    for j in range(CONV_W):
        y = y + ubuf[8 - (CONV_W - 1) + j:8 - (CONV_W - 1) + j + L, :] * cw_ref[j:j + 1, :]
    qk = y * jax.nn.sigmoid(y)

    gates = gt_ref[...] + bg_ref[...]
    col = lax.broadcasted_iota(jnp.int32, (L, LANES), 1)
    gates = jnp.where(col >= MLSTM_HEADS, _log_sigmoid(gates), gates)
    gates_t = gates.T

    t_idx = lax.broadcasted_iota(jnp.int32, (L, L), 0)
    s_idx = lax.broadcasted_iota(jnp.int32, (L, L), 1)
    causal = s_idx <= t_idx

    for h in range(MLSTM_HEADS):
        sl = slice(h * MLSTM_DH, (h + 1) * MLSTM_DH)
        qh = qk[:, sl]
        kh = qk[:, MLSTM_W + h * MLSTM_DH:MLSTM_W + (h + 1) * MLSTM_DH] * (MLSTM_DH ** -0.5)
        vh = v_ref[:, sl]
        i_col = gates[:, h:h + 1]
        f_col = gates[:, MLSTM_HEADS + h:MLSTM_HEADS + h + 1]
        i_row = gates_t[h:h + 1, :]
        f_row = gates_t[MLSTM_HEADS + h:MLSTM_HEADS + h + 1, :]
        m_prev = m_s[h:h + 1, 0:1]
        c_prev = c_s[h]
        n_prev = n_s[h:h + 1, :]

        b_col = jnp.sum(jnp.where(causal, f_row, 0.0), axis=1, keepdims=True)
        b_row = jnp.sum(jnp.where(t_idx <= s_idx, f_col, 0.0), axis=0, keepdims=True)
        dmat = jnp.where(causal, b_col - b_row + i_row, MASKED)
        a_col = b_col + m_prev
        mt = jnp.maximum(a_col, jnp.max(dmat, axis=1, keepdims=True))
        wa = jnp.exp(a_col - mt)
        qb = qh.astype(BF16)
        kb = kh.astype(BF16)
        vb = vh.astype(BF16)
        sc = _dot_nt(qb, kb) * jnp.exp(dmat - mt)
        num = wa * _dot(qb, c_prev.astype(BF16)) + _dot(sc.astype(BF16), vb)
        den = wa * jnp.sum(qh * n_prev, axis=1, keepdims=True) + jnp.sum(sc, axis=1, keepdims=True)
        hh = num / jnp.maximum(jnp.abs(den), jnp.exp(-mt))

        bl = b_row[:, L - 1:L]
        g_col = bl - b_col + i_col
        m_new = jnp.maximum(bl + m_prev, jnp.max(g_col, axis=0, keepdims=True))
        wc = jnp.exp(bl + m_prev - m_new)
        kw = jnp.exp(g_col - m_new) * kh
        c_s[h] = wc * c_prev + _dot_tn(kw.astype(BF16), vb)
        n_s[h:h + 1, :] = wc * n_prev + jnp.sum(kw, axis=0, keepdims=True)
        m_s[h:h + 1, :] = jnp.broadcast_to(m_new, (1, LANES))

        mu = jnp.mean(hh, axis=1, keepdims=True)
        var = jnp.mean(jnp.square(hh - mu), axis=1, keepdims=True)
        hn = (hh - mu) * lax.rsqrt(var + EPS) * gmh_ref[:, sl]
        hm_ref[:, sl] = hn * jax.nn.sigmoid(om_ref[:, sl])

    @pl.when(c == last)
    def _():
        conv_ref[0] = ubuf[L + 8 - (CONV_W - 1):L + 8, :]
        c_out[0] = c_s[...]
        n_out[0] = n_s[...]
        m_out[0] = m_s[...]


def _mlstm_prompt(z, gates, conv_w, conv_b, bg_row, g_mh, *, batch, seq, chunk):
    nc = seq // chunk
    n = batch * seq
    row = lambda b, c: b * nc + c
    W2 = 2 * MLSTM_W
    full = lambda shape: pl.BlockSpec(shape, lambda b, c: (0,) * len(shape))
    return pl.pallas_call(
        functools.partial(_mlstm_prompt_body, chunk=chunk),
        grid=(batch, nc),
        in_specs=[
            pl.BlockSpec((chunk, W2), lambda b, c: (row(b, c), COL_QK // W2)),
            pl.BlockSpec((chunk, MLSTM_W), lambda b, c: (row(b, c), COL_VM // MLSTM_W)),
            pl.BlockSpec((chunk, MLSTM_W), lambda b, c: (row(b, c), COL_OM // MLSTM_W)),
            pl.BlockSpec((chunk, LANES), lambda b, c: (row(b, c), 0)),
            full((CONV_W, W2)), full((1, W2)), full((1, LANES)), full((1, MLSTM_W)),
        ],
        out_specs=[
            pl.BlockSpec((chunk, MLSTM_W), lambda b, c: (row(b, c), 0)),
            pl.BlockSpec((1, CONV_W - 1, W2), lambda b, c: (b, 0, 0)),
            pl.BlockSpec((1, MLSTM_HEADS, MLSTM_DH, MLSTM_DH), lambda b, c: (b, 0, 0, 0)),
            pl.BlockSpec((1, MLSTM_HEADS, MLSTM_DH), lambda b, c: (b, 0, 0)),
            pl.BlockSpec((1, MLSTM_HEADS, LANES), lambda b, c: (b, 0, 0)),
        ],
        out_shape=[
            jax.ShapeDtypeStruct((n, MLSTM_W), F32),
            jax.ShapeDtypeStruct((batch, CONV_W - 1, W2), F32),
            jax.ShapeDtypeStruct((batch, MLSTM_HEADS, MLSTM_DH, MLSTM_DH), F32),
            jax.ShapeDtypeStruct((batch, MLSTM_HEADS, MLSTM_DH), F32),
            jax.ShapeDtypeStruct((batch, MLSTM_HEADS, LANES), F32),
        ],
        scratch_shapes=[
            pltpu.VMEM((chunk + 8, W2), F32),
            pltpu.VMEM((MLSTM_HEADS, MLSTM_DH, MLSTM_DH), F32),
            pltpu.VMEM((MLSTM_HEADS, MLSTM_DH), F32),
            pltpu.VMEM((MLSTM_HEADS, LANES), F32),
        ],
        compiler_params=_cparams(("arbitrary", "arbitrary"), 32),
        name="mlstm_prompt",
    )(z, z, z, gates, conv_w, conv_b, bg_row, g_mh)


def _dilated_prompt_body(q_ref, k_ref, v_ref, o_ref, kk, vv, acc, mm, ll, *, tq):
    i = pl.program_id(2)
    hd = ATTN_DH

    @pl.when(i == 0)
    def _():
        kk[0:tq, :] = jnp.zeros((tq, LANES), F32)
        vv[0:tq, :] = jnp.zeros((tq, LANES), F32)

    @pl.when(i > 0)
    def _():
        kk[0:tq, :] = kk[tq:2 * tq, :]
        vv[0:tq, :] = vv[tq:2 * tq, :]

    kk[tq:2 * tq, :] = k_ref[...]
    vv[tq:2 * tq, :] = v_ref[...]

    lane = lax.broadcasted_iota(jnp.int32, (REL, LANES), 1)
    first_head = lane < hd
    a_idx = lax.broadcasted_iota(jnp.int32, (REL, 2 * REL), 0)
    j_idx = lax.broadcasted_iota(jnp.int32, (REL, 2 * REL), 1)
    band = (j_idx >= a_idx) & (j_idx <= a_idx + REL)

    for bi, (window, d) in enumerate(DILATED):
        assert window // d == REL and tq % (REL * d) == 0

        def block(it, carry, d=d, bi=bi):
            r = lax.rem(it, d)
            cb = lax.div(it, d)
            q_start = r + d * REL * cb
            k_start = tq + r + d * REL * (cb - 1)
            rows = pl.ds(q_start, REL, stride=d)
            krows = pl.ds(k_start, 2 * REL, stride=d)
            q2 = q_ref[rows, :] * (hd ** -0.5)
            k2 = kk[krows, :].astype(BF16)
            v2 = vv[krows, :].astype(BF16)
            lo = jnp.where(jnp.logical_and(i == 0, cb == 0), REL, 0)
            valid = band & (j_idx >= lo)
            parts = []
            for hh in range(2):
                sel = first_head if hh == 0 else jnp.logical_not(first_head)
                qm = jnp.where(sel, q2, 0.0).astype(BF16)
                s = jnp.where(valid, _dot_nt(qm, k2), MASKED)
                mx = jnp.max(s, axis=1, keepdims=True)
                p = jnp.exp(s - mx)
                parts.append((mx, jnp.sum(p, axis=1, keepdims=True), _dot(p.astype(BF16), v2)))
            m_blk = jnp.where(first_head, parts[0][0], parts[1][0])
            l_blk = jnp.where(first_head, parts[0][1], parts[1][1])
            o_blk = jnp.where(first_head, parts[0][2], parts[1][2])
            if bi == 0:
                mm[rows, :] = m_blk
                ll[rows, :] = l_blk
                acc[rows, :] = o_blk
            else:
                m_old = mm[rows, :]
                m_new = jnp.maximum(m_old, m_blk)
                w_old = jnp.exp(m_old - m_new)
                w_blk = jnp.exp(m_blk - m_new)
                mm[rows, :] = m_new
                ll[rows, :] = ll[rows, :] * w_old + l_blk * w_blk
                acc[rows, :] = acc[rows, :] * w_old + o_blk * w_blk
            return carry

        lax.fori_loop(0, tq // REL, block, 0, unroll=4)

    o_ref[...] = acc[...] / ll[...]


def _dilated_prompt(z, *, batch, seq, tq):
    nt = seq // tq
    n = batch * seq
    pairs = ATTN_W // LANES

    def spec(col0):
        return pl.BlockSpec((tq, LANES), lambda b, p, i: (b * nt + i, col0 // LANES + p))

    return pl.pallas_call(
        functools.partial(_dilated_prompt_body, tq=tq),
        grid=(batch, pairs, nt),
        in_specs=[spec(COL_QA), spec(COL_KA), spec(COL_VA)],
        out_specs=pl.BlockSpec((tq, LANES), lambda b, p, i: (b * nt + i, p)),
        out_shape=jax.ShapeDtypeStruct((n, ATTN_W), F32),
        scratch_shapes=[
            pltpu.VMEM((2 * tq, LANES), F32), pltpu.VMEM((2 * tq, LANES), F32),
            pltpu.VMEM((tq, LANES), F32), pltpu.VMEM((tq, LANES), F32), pltpu.VMEM((tq, LANES), F32),
        ],
        compiler_params=_cparams(("arbitrary", "arbitrary", "arbitrary"), 32),
        name="dilated_prompt",
    )(z, z, z)


def _outproj_q_body(x_ref, hm_ref, ha_ref, wom_ref, woa_ref, gx_ref, wq_ref, x1_ref, q_ref):
    x1 = x_ref[...] + _dot(hm_ref[...].astype(BF16), wom_ref[...]) + _dot(ha_ref[...].astype(BF16), woa_ref[...])
    x1_ref[...] = x1
    q_ref[...] = _dot(_rms(x1, gx_ref[...]).astype(BF16), wq_ref[...]).astype(BF16)


def _outproj_q(x, hm, ha, w_out_m, w_out_a, g_x, w_mq, *, tm):
    n, d = x.shape
    row = lambda w: pl.BlockSpec((tm, w), lambda i: (i, 0))
    full = lambda shape: pl.BlockSpec(shape, lambda i: (0, 0))
    return pl.pallas_call(
        _outproj_q_body,
        grid=(n // tm,),
        in_specs=[row(d), row(MLSTM_W), row(ATTN_W), full((MLSTM_W, d)), full((ATTN_W, d)),
                  full((1, d)), full((d, d))],
        out_specs=[row(d), row(d)],
        out_shape=[jax.ShapeDtypeStruct((n, d), F32), jax.ShapeDtypeStruct((n, d), BF16)],
        compiler_params=_cparams(("arbitrary",), 40),
        name="outproj_q",
    )(x, hm, ha, w_out_m, w_out_a, g_x.reshape(1, d), w_mq)


def _xattn_prompt_body(q_ref, k_ref, v_ref, o_ref):
    for h in range(MEM_HEADS):
        sl = slice(h * MEM_DH, (h + 1) * MEM_DH)
        s = _dot_nt(q_ref[:, sl], k_ref[:, sl].astype(BF16)) * (MEM_DH ** -0.5)
        e = jnp.exp(s - jnp.max(s, axis=1, keepdims=True))
        p = e / jnp.sum(e, axis=1, keepdims=True)
        o_ref[:, sl] = _dot(p.astype(BF16), v_ref[:, sl].astype(BF16)).astype(BF16)


def _xattn_prompt(q, zmem, *, batch, seq, mem_len, tm):
    n, d = q.shape
    nt = seq // tm
    return pl.pallas_call(
        _xattn_prompt_body,
        grid=(batch, nt),
        in_specs=[
            pl.BlockSpec((tm, d), lambda b, i: (b * nt + i, 0)),
            pl.BlockSpec((mem_len, d), lambda b, i: (b, 0)),
            pl.BlockSpec((mem_len, d), lambda b, i: (b, 1)),
        ],
        out_specs=pl.BlockSpec((tm, d), lambda b, i: (b * nt + i, 0)),
        out_shape=jax.ShapeDtypeStruct((n, d), BF16),
        compiler_params=_cparams(("arbitrary", "arbitrary"), 32),
        name="xattn_prompt",
    )(q, zmem, zmem)


def _post_ffn_body(x1_ref, o_ref, wmo_ref, gf_ref, wg_ref, wu_ref, wd_ref, gfin_ref, y_ref, h_s, acc_s):
    j = pl.program_id(1)

    @pl.when(j == 0)
    def _():
        x2 = x1_ref[...] + _dot(o_ref[...], wmo_ref[...])
        acc_s[...] = x2
        h_s[...] = _rms(x2, gf_ref[...]).astype(BF16)

    h = h_s[...]
    g = _dot(h, wg_ref[...])
    u = _dot(h, wu_ref[...])
    acc_s[...] += _dot((g * jax.nn.sigmoid(g) * u).astype(BF16), wd_ref[...])

    @pl.when(j == pl.num_programs(1) - 1)
    def _():
        y_ref[...] = _rms(acc_s[...], gfin_ref[...])


def _post_ffn(x1, o, w_mo, g_ffn, w_gate, w_up, w_down, g_final, *, tm, tf):
    n, d = x1.shape
    dff = w_gate.shape[1]
    row = lambda: pl.BlockSpec((tm, d), lambda i, j: (i, 0))
    full = lambda shape: pl.BlockSpec(shape, lambda i, j: (0, 0))
    return pl.pallas_call(
        _post_ffn_body,
        grid=(n // tm, dff // tf),
        in_specs=[row(), row(), full((d, d)), full((1, d)),
                  pl.BlockSpec((d, tf), lambda i, j: (0, j)), pl.BlockSpec((d, tf), lambda i, j: (0, j)),
                  pl.BlockSpec((tf, d), lambda i, j: (j, 0)), full((1, d))],
        out_specs=row(),
        out_shape=jax.ShapeDtypeStruct((n, d), F32),
        scratch_shapes=[pltpu.VMEM((tm, d), BF16), pltpu.VMEM((tm, d), F32)],
        compiler_params=_cparams(("arbitrary", "arbitrary"), 48),
        name="post_ffn",
    )(x1, o, w_mo, g_ffn.reshape(1, d), w_gate, w_up, w_down, g_final.reshape(1, d))


def _mlstm_sample_body(qk_ref, v_ref, om_ref, gt_ref, conv_ref, c_ref, n_ref, m_ref,
                       cw_ref, cb_ref, bg_ref, gmh_ref,
                       hm_ref, conv_out, c_out, n_out, m_out, *, group, steps):
    T = steps
    W2 = 2 * MLSTM_W
    t_idx = lax.broadcasted_iota(jnp.int32, (T, T), 0)
    s_idx = lax.broadcasted_iota(jnp.int32, (T, T), 1)
    causal = s_idx <= t_idx
    eye = s_idx == t_idx
    col = lax.broadcasted_iota(jnp.int32, (T, LANES), 1)
    pad = jnp.zeros((8 - T, MLSTM_DH), F32)

    def to_row(v_col):
        return jnp.sum(jnp.where(eye, v_col, 0.0), axis=0, keepdims=True)

    def one(b, carry):
        up = jnp.concatenate([conv_ref[b], qk_ref[b]], axis=0)
        y = jnp.broadcast_to(cb_ref[...], (T, W2))
        for j in range(CONV_W):
            y = y + up[j:j + T, :] * cw_ref[j:j + 1, :]
        conv_out[b] = up[T:T + CONV_W - 1, :]
        qk = y * jax.nn.sigmoid(y)
        gates = gt_ref[b] + bg_ref[...]
        gates = jnp.where(col >= MLSTM_HEADS, _log_sigmoid(gates), gates)
        v_all = v_ref[b]
        om_all = om_ref[b]
        m_all = m_ref[b]
        hs = []
        for h in range(MLSTM_HEADS):
            sl = slice(h * MLSTM_DH, (h + 1) * MLSTM_DH)
            qh = qk[:, sl]
            kh = qk[:, MLSTM_W + h * MLSTM_DH:MLSTM_W + (h + 1) * MLSTM_DH] * (MLSTM_DH ** -0.5)
            vh = v_all[:, sl]
            i_col = gates[:, h:h + 1]
            f_col = gates[:, MLSTM_HEADS + h:MLSTM_HEADS + h + 1]
            i_row = to_row(i_col)
            f_row = to_row(f_col)
            m_prev = m_all[:, h:h + 1]
            c_prev = c_ref[b, h]
            n_prev = n_ref[b, h:h + 1, :]

            b_col = jnp.sum(jnp.where(causal, f_row, 0.0), axis=1, keepdims=True)
            b_row = to_row(b_col)
            dmat = jnp.where(causal, b_col - b_row + i_row, MASKED)
            a_col = b_col + m_prev
            mt = jnp.maximum(a_col, jnp.max(dmat, axis=1, keepdims=True))
            wa = jnp.exp(a_col - mt)
            decay = jnp.exp(dmat - mt)
            qkt = jnp.zeros((T, T), F32)
            for s in range(T):
                dots = jnp.sum(qh * kh[s:s + 1, :], axis=1, keepdims=True)
                qkt = jnp.where(s_idx == s, dots, qkt)
            sc = qkt * decay
            q8 = jnp.concatenate([qh, pad], axis=0).astype(BF16)
            inter = _dot(q8, c_prev.astype(BF16))[0:T, :]
            intra = jnp.zeros((T, MLSTM_DH), F32)
            for s in range(T):
                intra = intra + sc[:, s:s + 1] * vh[s:s + 1, :]
            num = wa * inter + intra
            den = wa * jnp.sum(qh * n_prev, axis=1, keepdims=True) + jnp.sum(sc, axis=1, keepdims=True)
            hh = num / jnp.maximum(jnp.abs(den), jnp.exp(-mt))

            bl = b_col[T - 1:T, :]
            g_col = bl - b_col + i_col
            m_new = jnp.maximum(bl + m_prev, jnp.max(g_col, axis=0, keepdims=True))
            wc = jnp.exp(bl + m_prev - m_new)
            kw = jnp.exp(g_col - m_new) * kh
            kw8 = jnp.concatenate([kw, pad], axis=0).astype(BF16)
            v8 = jnp.concatenate([vh, pad], axis=0).astype(BF16)
            c_out[b, h] = wc * c_prev + _dot_tn(kw8, v8)
            n_out[b, h:h + 1, :] = wc * n_prev + jnp.sum(kw, axis=0, keepdims=True)
            m_out[b, h:h + 1, :] = jnp.broadcast_to(m_new, (1, LANES))

            mu = jnp.mean(hh, axis=1, keepdims=True)
            var = jnp.mean(jnp.square(hh - mu), axis=1, keepdims=True)
            hn = (hh - mu) * lax.rsqrt(var + EPS) * gmh_ref[:, sl]
            hs.append(hn * jax.nn.sigmoid(om_all[:, sl]))
        hm_ref[b] = jnp.concatenate(hs, axis=1)
        return carry

    lax.fori_loop(0, group, one, 0, unroll=2)


def _mlstm_sample(z3, gates3, conv, c0, n0, m0, conv_w, conv_b, bg_row, g_mh, *, group):
    batch, steps, _ = z3.shape
    W2 = 2 * MLSTM_W
    full = lambda shape: pl.BlockSpec(shape, lambda g: (0,) * len(shape))
    return pl.pallas_call(
        functools.partial(_mlstm_sample_body, group=group, steps=steps),
        grid=(batch // group,),
        in_specs=[
            pl.BlockSpec((group, steps, W2), lambda g: (g, 0, COL_QK // W2)),
            pl.BlockSpec((group, steps, MLSTM_W), lambda g: (g, 0, COL_VM // MLSTM_W)),
            pl.BlockSpec((group, steps, MLSTM_W), lambda g: (g, 0, COL_OM // MLSTM_W)),
            pl.BlockSpec((group, steps, LANES), lambda g: (g, 0, 0)),
            pl.BlockSpec((group, CONV_W - 1, W2), lambda g: (g, 0, 0)),
            pl.BlockSpec((group, MLSTM_HEADS, MLSTM_DH, MLSTM_DH), lambda g: (g, 0, 0, 0)),
            pl.BlockSpec((group, MLSTM_HEADS, MLSTM_DH), lambda g: (g, 0, 0)),
            pl.BlockSpec((group, 1, LANES), lambda g: (g, 0, 0)),
            full((CONV_W, W2)), full((1, W2)), full((1, LANES)), full((1, MLSTM_W)),
        ],
        out_specs=[
            pl.BlockSpec((group, steps, MLSTM_W), lambda g: (g, 0, 0)),
            pl.BlockSpec((group, CONV_W - 1, W2), lambda g: (g, 0, 0)),
            pl.BlockSpec((group, MLSTM_HEADS, MLSTM_DH, MLSTM_DH), lambda g: (g, 0, 0, 0)),
            pl.BlockSpec((group, MLSTM_HEADS, MLSTM_DH), lambda g: (g, 0, 0)),
            pl.BlockSpec((group, MLSTM_HEADS, LANES), lambda g: (g, 0, 0)),
        ],
        out_shape=[
            jax.ShapeDtypeStruct((batch, steps, MLSTM_W), F32),
            jax.ShapeDtypeStruct((batch, CONV_W - 1, W2), F32),
            jax.ShapeDtypeStruct((batch, MLSTM_HEADS, MLSTM_DH, MLSTM_DH), F32),
            jax.ShapeDtypeStruct((batch, MLSTM_HEADS, MLSTM_DH), F32),
            jax.ShapeDtypeStruct((batch, MLSTM_HEADS, LANES), F32),
        ],
        compiler_params=_cparams(("arbitrary",), 32),
        name="mlstm_sample",
    )(z3, z3, z3, gates3, conv, c0, n0, m0, conv_w, conv_b, bg_row, g_mh)


def _branch_count(dist):
    cnt = jnp.zeros(dist.shape, F32)
    for window, d in DILATED:
        ok = (dist >= 0) & (dist <= window) & (lax.rem(jnp.maximum(dist, 0), d) == 0)
        cnt = cnt + jnp.where(ok, 1.0, 0.0)
    return cnt


def _dilated_sample_body(q_ref, kn_ref, vn_ref, kc_ref, vc_ref, o_ref, *, steps, wb):
    Q = q_ref.shape[1]
    t_c = lax.broadcasted_iota(jnp.int32, (Q, wb), 0)
    pos_c = lax.broadcasted_iota(jnp.int32, (Q, wb), 1)
    cnt_c = _branch_count(wb + t_c - pos_c)
    t_n = lax.broadcasted_iota(jnp.int32, (Q, Q), 0)
    s_n = lax.broadcasted_iota(jnp.int32, (Q, Q), 1)
    cnt_n = jnp.where(s_n < steps, _branch_count(t_n - s_n), 0.0)
    q_all = q_ref[0] * (ATTN_DH ** -0.5)
    kn_all = kn_ref[0]
    vn_all = vn_ref[0]
    outs = []
    for h in range(ATTN_HEADS):
        sl = slice(h * ATTN_DH, (h + 1) * ATTN_DH)
        qh = q_all[:, sl]
        knh = kn_all[:, sl]
        vnh = vn_all[:, sl]
        s_c = _dot(qh.astype(BF16), kc_ref[0, h].astype(BF16))
        s_new = jnp.zeros((Q, Q), F32)
        for s in range(steps):
            dots = jnp.sum(qh * knh[s:s + 1, :], axis=1, keepdims=True)
            s_new = jnp.where(s_n == s, dots, s_new)
        s_c = jnp.where(cnt_c > 0, s_c, MASKED)
        s_new = jnp.where(cnt_n > 0, s_new, MASKED)
        mx = jnp.maximum(jnp.max(s_c, axis=1, keepdims=True), jnp.max(s_new, axis=1, keepdims=True))
        p_c = cnt_c * jnp.exp(s_c - mx)
        p_n = cnt_n * jnp.exp(s_new - mx)
        den = jnp.sum(p_c, axis=1, keepdims=True) + jnp.sum(p_n, axis=1, keepdims=True)
        o = _dot_nt(p_c.astype(BF16), vc_ref[0, h].astype(BF16))
        for s in range(steps):
            o = o + p_n[:, s:s + 1] * vnh[s:s + 1, :]
        outs.append(o / den)
    o_ref[0] = jnp.concatenate(outs, axis=1)


def _dilated_sample(q8, kn8, vn8, kc_t, vc_t, *, steps):
    batch, qrows, _ = q8.shape
    wb = kc_t.shape[-1]
    new = lambda: pl.BlockSpec((1, qrows, ATTN_W), lambda b: (b, 0, 0))
    cache = lambda: pl.BlockSpec((1, ATTN_HEADS, ATTN_DH, wb), lambda b: (b, 0, 0, 0))
    return pl.pallas_call(
        functools.partial(_dilated_sample_body, steps=steps, wb=wb),
        grid=(batch,),
        in_specs=[new(), new(), new(), cache(), cache()],
        out_specs=new(),
        out_shape=jax.ShapeDtypeStruct((batch, qrows, ATTN_W), F32),
        compiler_params=_cparams(("arbitrary",), 40),
        name="dilated_sample",
    )(q8, kn8, vn8, kc_t, vc_t)


def _xattn_sample_body(q_ref, k_ref, v_ref, o_ref):
    Q = q_ref.shape[1]
    mem_len = k_ref.shape[1]
    rows = mem_len * MEM_HEADS
    k2 = k_ref[0].reshape(rows, MEM_DH).astype(BF16)
    v2 = v_ref[0].reshape(rows, MEM_DH).astype(BF16)
    q_all = q_ref[0]
    qs = jnp.concatenate([q_all[:, h * MEM_DH:(h + 1) * MEM_DH] for h in range(MEM_HEADS)], axis=0)
    s = _dot_nt(qs, k2) * (MEM_DH ** -0.5)
    q_head = lax.div(lax.broadcasted_iota(jnp.int32, s.shape, 0), Q)
    k_head = lax.rem(lax.broadcasted_iota(jnp.int32, s.shape, 1), MEM_HEADS)
    s = jnp.where(q_head == k_head, s, MASKED)
    e = jnp.exp(s - jnp.max(s, axis=1, keepdims=True))
    p = e / jnp.sum(e, axis=1, keepdims=True)
    o = _dot(p.astype(BF16), v2)
    for h in range(MEM_HEADS):
        o_ref[0, :, h * MEM_DH:(h + 1) * MEM_DH] = o[h * Q:(h + 1) * Q, :].astype(BF16)


def _xattn_sample(q8, mem_k, mem_v):
    batch, qrows, d = q8.shape
    mem_len = mem_k.shape[1]
    mem = lambda: pl.BlockSpec((1, mem_len, MEM_HEADS, MEM_DH), lambda b: (b, 0, 0, 0))
    return pl.pallas_call(
        _xattn_sample_body,
        grid=(batch,),
        in_specs=[pl.BlockSpec((1, qrows, d), lambda b: (b, 0, 0)), mem(), mem()],
        out_specs=pl.BlockSpec((1, qrows, d), lambda b: (b, 0, 0)),
        out_shape=jax.ShapeDtypeStruct((batch, qrows, d), BF16),
        compiler_params=_cparams(("arbitrary",), 32),
        name="xattn_sample",
    )(q8, mem_k, mem_v)


def _pad_rows(a, rows):
    return jnp.pad(a, ((0, 0), (0, rows - a.shape[1]), (0, 0)))


def kernel(x_prompt, x_sample, mem_prompt, cache_attn_k, cache_attn_v, cache_mem_k, cache_mem_v, state_conv, state_C, state_n, state_m, g_mix, w_in, conv_w, conv_b, b_gates, g_mh, w_out, g_mem, w_mk, w_mv, g_xattn, w_mq, w_mo, g_ffn, w_gate, w_up, w_down, g_final):
    B, T, D = x_prompt.shape
    SB, ST, _ = x_sample.shape
    mem_len = mem_prompt.shape[1]
    W2 = 2 * MLSTM_W
    gate0 = COL_QA
    n_gate = 2 * MLSTM_HEADS

    w_main = jnp.concatenate([w_in[:, :gate0], w_in[:, gate0 + n_gate:]], axis=1).astype(BF16)
    w_gcols = jnp.pad(w_in[:, gate0:gate0 + n_gate], ((0, 0), (0, LANES - n_gate))).astype(BF16)
    bg_row = jnp.pad(b_gates, (0, LANES - n_gate)).reshape(1, LANES)
    w_mem = jnp.concatenate([w_mk, w_mv], axis=1).astype(BF16)
    w_out_m = w_out[:MLSTM_W].astype(BF16)
    w_out_a = w_out[MLSTM_W:].astype(BF16)
    w_mq_b, w_mo_b = w_mq.astype(BF16), w_mo.astype(BF16)
    w_gate_b, w_up_b, w_down_b = w_gate.astype(BF16), w_up.astype(BF16), w_down.astype(BF16)
    conv_b2 = conv_b.reshape(1, W2)
    g_mh2 = g_mh.reshape(1, MLSTM_W)

    xp = x_prompt.reshape(B * T, D)
    zmem = _norm_matmul(mem_prompt.reshape(B * mem_len, D), g_mem, w_mem, tm=B * mem_len, tn=512)
    z, gates = _norm_matmul(xp, g_mix, w_main, w_gcols, tm=1024, tn=512)
    hm, p_conv, p_C, p_n, p_m = _mlstm_prompt(z, gates, conv_w, conv_b2, bg_row, g_mh2,
                                              batch=B, seq=T, chunk=128)
    ha = _dilated_prompt(z, batch=B, seq=T, tq=2048)
    x1, q = _outproj_q(xp, hm, ha, w_out_m, w_out_a, g_xattn, w_mq_b, tm=512)
    o = _xattn_prompt(q, zmem, batch=B, seq=T, mem_len=mem_len, tm=512)
    y_prompt = _post_ffn(x1, o, w_mo_b, g_ffn, w_gate_b, w_up_b, w_down_b, g_final, tm=1024, tf=256)

    keep = min(DILATED[-1][0], T)
    z3p = z.reshape(B, T, N_MAIN)
    p_attn_k = z3p[:, T - keep:, COL_KA:COL_KA + ATTN_W].reshape(B, keep, ATTN_HEADS, ATTN_DH)
    p_attn_v = z3p[:, T - keep:, COL_VA:COL_VA + ATTN_W].reshape(B, keep, ATTN_HEADS, ATTN_DH)
    p_mem_k = zmem[:, :D].reshape(B, mem_len, MEM_HEADS, MEM_DH)
    p_mem_v = zmem[:, D:].reshape(B, mem_len, MEM_HEADS, MEM_DH)

    xs = x_sample.reshape(SB * ST, D)
    zs, gates_s = _norm_matmul(xs, g_mix, w_main, w_gcols, tm=SB * ST, tn=512)
    zs3 = zs.reshape(SB, ST, N_MAIN)
    m0 = jnp.pad(state_m, ((0, 0), (0, LANES - MLSTM_HEADS))).reshape(SB, 1, LANES)
    hm_s, s_conv, s_C, s_n, s_m = _mlstm_sample(
        zs3, gates_s.reshape(SB, ST, LANES), state_conv, state_C, state_n, m0,
        conv_w, conv_b2, bg_row, g_mh2, group=8)
    qrows = 8
    qa8 = _pad_rows(zs3[:, :, COL_QA:COL_QA + ATTN_W], qrows)
    ka8 = _pad_rows(zs3[:, :, COL_KA:COL_KA + ATTN_W], qrows)
    va8 = _pad_rows(zs3[:, :, COL_VA:COL_VA + ATTN_W], qrows)
    ha_s = _dilated_sample(qa8, ka8, va8, jnp.transpose(cache_attn_k, (0, 2, 3, 1)),
                           jnp.transpose(cache_attn_v, (0, 2, 3, 1)), steps=ST)[:, :ST]
    x1s, qs = _outproj_q(xs, hm_s.reshape(SB * ST, MLSTM_W), ha_s.reshape(SB * ST, ATTN_W),
                         w_out_m, w_out_a, g_xattn, w_mq_b, tm=SB * ST)
    os_ = _xattn_sample(_pad_rows(qs.reshape(SB, ST, D), qrows), cache_mem_k, cache_mem_v)[:, :ST]
    y_sample = _post_ffn(x1s, os_.reshape(SB * ST, D), w_mo_b, g_ffn, w_gate_b, w_up_b, w_down_b,
                         g_final, tm=SB * ST, tf=256)

    s_attn_k = zs3[:, :, COL_KA:COL_KA + ATTN_W].reshape(SB, ST, ATTN_HEADS, ATTN_DH)
    s_attn_v = zs3[:, :, COL_VA:COL_VA + ATTN_W].reshape(SB, ST, ATTN_HEADS, ATTN_DH)

    return (y_prompt.reshape(B, T, D), y_sample.reshape(SB, ST, D), p_attn_k, p_attn_v,
            p_conv, p_C, p_n, p_m[:, :, 0], p_mem_k, p_mem_v,
            s_attn_k, s_attn_v, s_conv, s_C, s_n, s_m[:, :, 0])
```

```python
import functools

import jax
import jax.numpy as jnp
from jax import lax
from jax.experimental import pallas as pl
from jax.experimental.pallas import tpu as pltpu

F32 = jnp.float32
BF16 = jnp.bfloat16

EPS = 1e-6
MASKED = -1e30

MLSTM_HEADS = 4
MLSTM_DH = 128
MLSTM_W = MLSTM_HEADS * MLSTM_DH
ATTN_HEADS = 8
ATTN_DH = 64
ATTN_W = ATTN_HEADS * ATTN_DH
DILATED = ((128, 1), (512, 4), (2048, 16))
REL = 128
CONV_W = 4
MEM_HEADS = 4
MEM_DH = 256
LANES = 128
MIB = 1024 * 1024
PAD_PITCH = 24

COL_QK = 0
COL_VM = 2 * MLSTM_W
COL_OM = 3 * MLSTM_W
COL_QA = 4 * MLSTM_W
COL_KA = COL_QA + ATTN_W
COL_VA = COL_KA + ATTN_W
N_MAIN = COL_VA + ATTN_W


def _cparams(semantics, vmem_mib):
    return pltpu.CompilerParams(dimension_semantics=semantics, vmem_limit_bytes=vmem_mib * MIB)


def _rms(x, g):
    return x * lax.rsqrt(jnp.mean(x * x, axis=-1, keepdims=True) + EPS) * g


def _log_sigmoid(x):
    return jnp.minimum(x, 0.0) - jnp.log1p(jnp.exp(-jnp.abs(x)))


def _dot(a, b):
    return jnp.dot(a, b, preferred_element_type=F32)


def _dot_nt(a, b):
    return lax.dot_general(a, b, (((1,), (1,)), ((), ())), preferred_element_type=F32)


def _dot_tn(a, b):
    return lax.dot_general(a, b, (((0,), (0,)), ((), ())), preferred_element_type=F32)


def _norm_matmul_body(x_ref, g_ref, w_ref, *rest, with_side):
    if with_side:
        ws_ref, z_ref, side_ref, h_ref = rest
    else:
        z_ref, h_ref = rest

    @pl.when(pl.program_id(1) == 0)
    def _():
        h = _rms(x_ref[...], g_ref[...]).astype(BF16)
        h_ref[...] = h
        if with_side:
            side_ref[...] = _dot(h, ws_ref[...])

    z_ref[...] = _dot(h_ref[...], w_ref[...])


def _norm_matmul(x, g, w, w_side=None, *, tm, tn):
    n, d = x.shape
    c = w.shape[1]
    with_side = w_side is not None
    in_specs = [
        pl.BlockSpec((tm, d), lambda i, j: (i, 0)),
        pl.BlockSpec((1, d), lambda i, j: (0, 0)),
        pl.BlockSpec((d, tn), lambda i, j: (0, j)),
    ]
    out_specs = [pl.BlockSpec((tm, tn), lambda i, j: (i, j))]
    out_shape = [jax.ShapeDtypeStruct((n, c), F32)]
    args = [x, g.reshape(1, d), w]
    if with_side:
        in_specs.append(pl.BlockSpec((d, LANES), lambda i, j: (0, 0)))
        out_specs.append(pl.BlockSpec((tm, LANES), lambda i, j: (i, 0)))
        out_shape.append(jax.ShapeDtypeStruct((n, LANES), F32))
        args.append(w_side)
    vmem = 2 * tm * d * 4 + tm * d * 2 + 2 * d * tn * 2 + 2 * tm * tn * 4 + 4 * tm * LANES * 4
    res = pl.pallas_call(
        functools.partial(_norm_matmul_body, with_side=with_side),
        grid=(n // tm, c // tn),
        in_specs=in_specs,
        out_specs=out_specs,
        out_shape=out_shape,
        scratch_shapes=[pltpu.VMEM((tm, d), BF16)],
        compiler_params=_cparams(("arbitrary", "arbitrary"), vmem // MIB + 8),
        name="norm_matmul",
    )(*args)
    return res if with_side else res[0]


def _mlstm_prompt_body(qk_ref, v_ref, om_ref, gt_ref, cw_ref, cb_ref, bg_ref, gmh_ref,
                       hm_ref, conv_ref, c_out, n_out, m_out,
                       ubuf, c_s, n_s, m_s, *, chunk):
    L = chunk
    c = pl.program_id(1)
    last = pl.num_programs(1) - 1
    W2 = 2 * MLSTM_W

    @pl.when(c == 0)
    def _():
        ubuf[0:8, :] = jnp.zeros((8, W2), F32)
        c_s[...] = jnp.zeros_like(c_s)
        n_s[...] = jnp.zeros_like(n_s)
        m_s[...] = jnp.zeros_like(m_s)

    @pl.when(c > 0)
    def _():
        ubuf[0:8, :] = ubuf[L:L + 8, :]

    ubuf[8:L + 8, :] = qk_ref[...]
    y = jnp.broadcast_to(cb_ref[...], (L, W2))
    for j in range(CONV_W):
        y = y + ubuf[8 - (CONV_W - 1) + j:8 - (CONV_W - 1) + j + L, :] * cw_ref[j:j + 1, :]
    qk = y * jax.nn.sigmoid(y)

    gates = gt_ref[...] + bg_ref[...]
    col = lax.broadcasted_iota(jnp.int32, (L, LANES), 1)
    gates = jnp.where(col >= MLSTM_HEADS, _log_sigmoid(gates), gates)
    gates_t = gates.T

    t_idx = lax.broadcasted_iota(jnp.int32, (L, L), 0)
    s_idx = lax.broadcasted_iota(jnp.int32, (L, L), 1)
    causal = s_idx <= t_idx

    for h in range(MLSTM_HEADS):
        sl = slice(h * MLSTM_DH, (h + 1) * MLSTM_DH)
        qh = qk[:, sl]
        kh = qk[:, MLSTM_W + h * MLSTM_DH:MLSTM_W + (h + 1) * MLSTM_DH] * (MLSTM_DH ** -0.5)
        vh = v_ref[:, sl]
        i_col = gates[:, h:h + 1]
        f_col = gates[:, MLSTM_HEADS + h:MLSTM_HEADS + h + 1]
        i_row = gates_t[h:h + 1, :]
        f_row = gates_t[MLSTM_HEADS + h:MLSTM_HEADS + h + 1, :]
        m_prev = m_s[h:h + 1, 0:1]
        c_prev = c_s[h]
        n_prev = n_s[h:h + 1, :]

        b_col = jnp.sum(jnp.where(causal, f_row, 0.0), axis=1, keepdims=True)
        b_row = jnp.sum(jnp.where(t_idx <= s_idx, f_col, 0.0), axis=0, keepdims=True)
        dmat = jnp.where(causal, b_col - b_row + i_row, MASKED)
        a_col = b_col + m_prev
        mt = jnp.maximum(a_col, jnp.max(dmat, axis=1, keepdims=True))
        wa = jnp.exp(a_col - mt)
        qb = qh.astype(BF16)
        kb = kh.astype(BF16)
        vb = vh.astype(BF16)
        sc = _dot_nt(qb, kb) * jnp.exp(dmat - mt)
        num = wa * _dot(qb, c_prev.astype(BF16)) + _dot(sc.astype(BF16), vb)
        den = wa * jnp.sum(qh * n_prev, axis=1, keepdims=True) + jnp.sum(sc, axis=1, keepdims=True)
        hh = num / jnp.maximum(jnp.abs(den), jnp.exp(-mt))

        bl = b_row[:, L - 1:L]
        g_col = bl - b_col + i_col
        m_new = jnp.maximum(bl + m_prev, jnp.max(g_col, axis=0, keepdims=True))
        wc = jnp.exp(bl + m_prev - m_new)
        kw = jnp.exp(g_col - m_new) * kh
        c_s[h] = wc * c_prev + _dot_tn(kw.astype(BF16), vb)
        n_s[h:h + 1, :] = wc * n_prev + jnp.sum(kw, axis=0, keepdims=True)
        m_s[h:h + 1, :] = jnp.broadcast_to(m_new, (1, LANES))

        mu = jnp.mean(hh, axis=1, keepdims=True)
        var = jnp.mean(jnp.square(hh - mu), axis=1, keepdims=True)
        hn = (hh - mu) * lax.rsqrt(var + EPS) * gmh_ref[:, sl]
        hm_ref[:, sl] = hn * jax.nn.sigmoid(om_ref[:, sl])

    @pl.when(c == last)
    def _():
        conv_ref[0] = ubuf[L + 8 - (CONV_W - 1):L + 8, :]
        c_out[0] = c_s[...]
        n_out[0] = n_s[...]
        m_out[0] = m_s[...]


def _mlstm_prompt(z, gates, conv_w, conv_b, bg_row, g_mh, *, batch, seq, chunk):
    nc = seq // chunk
    n = batch * seq
    row = lambda b, c: b * nc + c
    W2 = 2 * MLSTM_W
    full = lambda shape: pl.BlockSpec(shape, lambda b, c: (0,) * len(shape))
    return pl.pallas_call(
        functools.partial(_mlstm_prompt_body, chunk=chunk),
        grid=(batch, nc),
        in_specs=[
            pl.BlockSpec((chunk, W2), lambda b, c: (row(b, c), COL_QK // W2)),
            pl.BlockSpec((chunk, MLSTM_W), lambda b, c: (row(b, c), COL_VM // MLSTM_W)),
            pl.BlockSpec((chunk, MLSTM_W), lambda b, c: (row(b, c), COL_OM // MLSTM_W)),
            pl.BlockSpec((chunk, LANES), lambda b, c: (row(b, c), 0)),
            full((CONV_W, W2)), full((1, W2)), full((1, LANES)), full((1, MLSTM_W)),
        ],
        out_specs=[
            pl.BlockSpec((chunk, MLSTM_W), lambda b, c: (row(b, c), 0)),
            pl.BlockSpec((1, CONV_W - 1, W2), lambda b, c: (b, 0, 0)),
            pl.BlockSpec((1, MLSTM_HEADS, MLSTM_DH, MLSTM_DH), lambda b, c: (b, 0, 0, 0)),
            pl.BlockSpec((1, MLSTM_HEADS, MLSTM_DH), lambda b, c: (b, 0, 0)),
            pl.BlockSpec((1, MLSTM_HEADS, LANES), lambda b, c: (b, 0, 0)),
        ],
        out_shape=[
            jax.ShapeDtypeStruct((n, MLSTM_W), F32),
            jax.ShapeDtypeStruct((batch, CONV_W - 1, W2), F32),
            jax.ShapeDtypeStruct((batch, MLSTM_HEADS, MLSTM_DH, MLSTM_DH), F32),
            jax.ShapeDtypeStruct((batch, MLSTM_HEADS, MLSTM_DH), F32),
            jax.ShapeDtypeStruct((batch, MLSTM_HEADS, LANES), F32),
        ],
        scratch_shapes=[
            pltpu.VMEM((chunk + 8, W2), F32),
            pltpu.VMEM((MLSTM_HEADS, MLSTM_DH, MLSTM_DH), F32),
            pltpu.VMEM((MLSTM_HEADS, MLSTM_DH), F32),
            pltpu.VMEM((MLSTM_HEADS, LANES), F32),
        ],
        compiler_params=_cparams(("arbitrary", "arbitrary"), 32),
        name="mlstm_prompt",
    )(z, z, z, gates, conv_w, conv_b, bg_row, g_mh)


def _dilated_prompt_body(q_ref, k_ref, kp_ref, v_ref, vp_ref, o_ref, kpad, vpad, acc, mm, ll, bias, *, tq):
    i = pl.program_id(2)
    hd = ATTN_DH
    dmax = max(d for _, d in DILATED)
    ngrp = tq // dmax
    half = ngrp * PAD_PITCH
    cur = lax.rem(i, 2) * half
    prev = lax.rem(i + 1, 2) * half

    @pl.when(i == 0)
    def _():
        kpad[pl.ds(pl.multiple_of(prev, 8), half), :] = jnp.zeros((half, LANES), F32)
        vpad[pl.ds(pl.multiple_of(prev, 8), half), :] = jnp.zeros((half, LANES), F32)

    def pad_copy(g, carry):
        src = pl.ds(pl.multiple_of(g * dmax, dmax), dmax)
        dst = pl.ds(pl.multiple_of(cur + g * PAD_PITCH, 8), dmax)
        kpad[dst, :] = k_ref[src, :]
        vpad[dst, :] = v_ref[src, :]
        return carry

    lax.fori_loop(0, ngrp, pad_copy, 0, unroll=8)

    lane = lax.broadcasted_iota(jnp.int32, (REL, LANES), 1)
    first_head = lane < hd
    a_idx = lax.broadcasted_iota(jnp.int32, (REL, 2 * REL), 0)
    j_idx = lax.broadcasted_iota(jnp.int32, (REL, 2 * REL), 1)
    band = (j_idx >= a_idx) & (j_idx <= a_idx + REL)
    b0 = jnp.where(band, 0.0, MASKED)
    b1 = jnp.where(band & (j_idx >= jnp.where(i == 0, REL, 0)), 0.0, MASKED)
    bias[0] = jnp.concatenate([b0, b0], axis=0)
    bias[1] = jnp.concatenate([b1, b1], axis=0)

    order = sorted(DILATED, key=lambda wd: -wd[1])
    for bi, (window, d) in enumerate(order):
        assert window // d == REL and tq % (REL * d) == 0
        nblk = tq // (REL * d)
        for r in range(d):
            for cb in range(nblk):
                rows = pl.ds(r + d * REL * cb, REL, stride=d)
                q2 = q_ref[rows, :] * (hd ** -0.5)
                if d == dmax:
                    kp = kpad[pl.ds(prev + r, REL, stride=PAD_PITCH), :]
                    kc = kpad[pl.ds(cur + r, REL, stride=PAD_PITCH), :]
                    vp = vpad[pl.ds(prev + r, REL, stride=PAD_PITCH), :]
                    vc = vpad[pl.ds(cur + r, REL, stride=PAD_PITCH), :]
                    k2 = jnp.concatenate([kp, kc], axis=0)
                    v2 = jnp.concatenate([vp, vc], axis=0)
                elif cb == 0:
                    tail = pl.ds(r + d * REL * (nblk - 1), REL, stride=d)
                    k2 = jnp.concatenate([kp_ref[tail, :], k_ref[rows, :]], axis=0)
                    v2 = jnp.concatenate([vp_ref[tail, :], v_ref[rows, :]], axis=0)
                else:
                    both = pl.ds(r + d * REL * (cb - 1), 2 * REL, stride=d)
                    k2 = k_ref[both, :]
                    v2 = v_ref[both, :]
                qs = jnp.concatenate([jnp.where(first_head, q2, 0.0), jnp.where(first_head, 0.0, q2)], axis=0)
                s = _dot_nt(qs.astype(BF16), k2.astype(BF16)) + bias[1 if cb == 0 else 0]
                mx = jnp.max(s, axis=1, keepdims=True)
                p = jnp.exp(s - mx)
                l = jnp.sum(p, axis=1, keepdims=True)
                o = _dot(p.astype(BF16), v2.astype(BF16))
                m_blk = jnp.where(first_head, mx[0:REL], mx[REL:2 * REL])
                l_blk = jnp.where(first_head, l[0:REL], l[REL:2 * REL])
                o_blk = jnp.where(first_head, o[0:REL], o[REL:2 * REL])
                if bi == 0:
                    mm[rows, :] = m_blk
                    ll[rows, :] = l_blk
                    acc[rows, :] = o_blk
                else:
                    m_old = mm[rows, :]
                    m_new = jnp.maximum(m_old, m_blk)
                    w_old = jnp.exp(m_old - m_new)
                    w_blk = jnp.exp(m_blk - m_new)
                    mm[rows, :] = m_new
                    ll[rows, :] = ll[rows, :] * w_old + l_blk * w_blk
                    acc[rows, :] = acc[rows, :] * w_old + o_blk * w_blk

    o_ref[...] = acc[...] / ll[...]


def _dilated_prompt(z, *, batch, seq, tq):
    nt = seq // tq
    n = batch * seq
    pairs = ATTN_W // LANES
    dmax = max(d for _, d in DILATED)
    assert tq == REL * dmax
    pad_rows = 2 * (tq // dmax) * PAD_PITCH

    def spec(col0, back):
        return pl.BlockSpec((tq, LANES),
                            lambda b, p, i: (b * nt + jnp.maximum(i - back, 0), col0 // LANES + p))

    return pl.pallas_call(
        functools.partial(_dilated_prompt_body, tq=tq),
        grid=(batch, pairs, nt),
        in_specs=[spec(COL_QA, 0), spec(COL_KA, 0), spec(COL_KA, 1), spec(COL_VA, 0), spec(COL_VA, 1)],
        out_specs=pl.BlockSpec((tq, LANES), lambda b, p, i: (b * nt + i, p)),
        out_shape=jax.ShapeDtypeStruct((n, ATTN_W), F32),
        scratch_shapes=[
            pltpu.VMEM((pad_rows, LANES), F32), pltpu.VMEM((pad_rows, LANES), F32),
            pltpu.VMEM((tq, LANES), F32), pltpu.VMEM((tq, LANES), F32), pltpu.VMEM((tq, LANES), F32),
            pltpu.VMEM((2, 2 * REL, 2 * REL), F32),
        ],
        compiler_params=_cparams(("arbitrary", "arbitrary", "arbitrary"), 32),
        name="dilated_prompt",
    )(z, z, z, z, z)


def _outproj_q_body(x_ref, hm_ref, ha_ref, wom_ref, woa_ref, gx_ref, wq_ref, x1_ref, q_ref):
    x1 = x_ref[...] + _dot(hm_ref[...].astype(BF16), wom_ref[...]) + _dot(ha_ref[...].astype(BF16), woa_ref[...])
    x1_ref[...] = x1
    q_ref[...] = _dot(_rms(x1, gx_ref[...]).astype(BF16), wq_ref[...]).astype(BF16)


def _outproj_q(x, hm, ha, w_out_m, w_out_a, g_x, w_mq, *, tm):
    n, d = x.shape
    row = lambda w: pl.BlockSpec((tm, w), lambda i: (i, 0))
    full = lambda shape: pl.BlockSpec(shape, lambda i: (0, 0))
    return pl.pallas_call(
        _outproj_q_body,
        grid=(n // tm,),
        in_specs=[row(d), row(MLSTM_W), row(ATTN_W), full((MLSTM_W, d)), full((ATTN_W, d)),
                  full((1, d)), full((d, d))],
        out_specs=[row(d), row(d)],
        out_shape=[jax.ShapeDtypeStruct((n, d), F32), jax.ShapeDtypeStruct((n, d), BF16)],
        compiler_params=_cparams(("arbitrary",), 40),
        name="outproj_q",
    )(x, hm, ha, w_out_m, w_out_a, g_x.reshape(1, d), w_mq)


def _xattn_prompt_body(q_ref, k_ref, v_ref, o_ref):
    for h in range(MEM_HEADS):
        sl = slice(h * MEM_DH, (h + 1) * MEM_DH)
        s = _dot_nt(q_ref[:, sl], k_ref[:, sl].astype(BF16)) * (MEM_DH ** -0.5)
        e = jnp.exp(s - jnp.max(s, axis=1, keepdims=True))
        p = e / jnp.sum(e, axis=1, keepdims=True)
        o_ref[:, sl] = _dot(p.astype(BF16), v_ref[:, sl].astype(BF16)).astype(BF16)


def _xattn_prompt(q, zmem, *, batch, seq, mem_len, tm):
    n, d = q.shape
    nt = seq // tm
    return pl.pallas_call(
        _xattn_prompt_body,
        grid=(batch, nt),
        in_specs=[
            pl.BlockSpec((tm, d), lambda b, i: (b * nt + i, 0)),
            pl.BlockSpec((mem_len, d), lambda b, i: (b, 0)),
            pl.BlockSpec((mem_len, d), lambda b, i: (b, 1)),
        ],
        out_specs=pl.BlockSpec((tm, d), lambda b, i: (b * nt + i, 0)),
        out_shape=jax.ShapeDtypeStruct((n, d), BF16),
        compiler_params=_cparams(("arbitrary", "arbitrary"), 32),
        name="xattn_prompt",
    )(q, zmem, zmem)


def _post_ffn_body(x1_ref, o_ref, wmo_ref, gf_ref, wg_ref, wu_ref, wd_ref, gfin_ref, y_ref, h_s, acc_s):
    j = pl.program_id(1)

    @pl.when(j == 0)
    def _():
        x2 = x1_ref[...] + _dot(o_ref[...], wmo_ref[...])
        acc_s[...] = x2
        h_s[...] = _rms(x2, gf_ref[...]).astype(BF16)

    h = h_s[...]
    g = _dot(h, wg_ref[...])
    u = _dot(h, wu_ref[...])
    acc_s[...] += _dot((g * jax.nn.sigmoid(g) * u).astype(BF16), wd_ref[...])

    @pl.when(j == pl.num_programs(1) - 1)
    def _():
        y_ref[...] = _rms(acc_s[...], gfin_ref[...])


def _post_ffn(x1, o, w_mo, g_ffn, w_gate, w_up, w_down, g_final, *, tm, tf):
    n, d = x1.shape
    dff = w_gate.shape[1]
    row = lambda: pl.BlockSpec((tm, d), lambda i, j: (i, 0))
    full = lambda shape: pl.BlockSpec(shape, lambda i, j: (0, 0))
    return pl.pallas_call(
        _post_ffn_body,
        grid=(n // tm, dff // tf),
        in_specs=[row(), row(), full((d, d)), full((1, d)),
                  pl.BlockSpec((d, tf), lambda i, j: (0, j)), pl.BlockSpec((d, tf), lambda i, j: (0, j)),
                  pl.BlockSpec((tf, d), lambda i, j: (j, 0)), full((1, d))],
        out_specs=row(),
        out_shape=jax.ShapeDtypeStruct((n, d), F32),
        scratch_shapes=[pltpu.VMEM((tm, d), BF16), pltpu.VMEM((tm, d), F32)],
        compiler_params=_cparams(("arbitrary", "arbitrary"), 48),
        name="post_ffn",
    )(x1, o, w_mo, g_ffn.reshape(1, d), w_gate, w_up, w_down, g_final.reshape(1, d))


def _mlstm_sample_body(qk_ref, v_ref, om_ref, gt_ref, conv_ref, c_ref, n_ref, m_ref,
                       cw_ref, cb_ref, bg_ref, gmh_ref,
                       hm_ref, conv_out, c_out, n_out, m_out, *, group, steps):
    T = steps
    W2 = 2 * MLSTM_W
    t_idx = lax.broadcasted_iota(jnp.int32, (T, T), 0)
    s_idx = lax.broadcasted_iota(jnp.int32, (T, T), 1)
    causal = s_idx <= t_idx
    eye = s_idx == t_idx
    col = lax.broadcasted_iota(jnp.int32, (T, LANES), 1)
    pad = jnp.zeros((8 - T, MLSTM_DH), F32)

    def to_row(v_col):
        return jnp.sum(jnp.where(eye, v_col, 0.0), axis=0, keepdims=True)

    def one(b, carry):
        up = jnp.concatenate([conv_ref[b], qk_ref[b]], axis=0)
        y = jnp.broadcast_to(cb_ref[...], (T, W2))
        for j in range(CONV_W):
            y = y + up[j:j + T, :] * cw_ref[j:j + 1, :]
        conv_out[b] = up[T:T + CONV_W - 1, :]
        qk = y * jax.nn.sigmoid(y)
        gates = gt_ref[b] + bg_ref[...]
        gates = jnp.where(col >= MLSTM_HEADS, _log_sigmoid(gates), gates)
        v_all = v_ref[b]
        om_all = om_ref[b]
        m_all = m_ref[b]
        hs = []
        for h in range(MLSTM_HEADS):
            sl = slice(h * MLSTM_DH, (h + 1) * MLSTM_DH)
            qh = qk[:, sl]
            kh = qk[:, MLSTM_W + h * MLSTM_DH:MLSTM_W + (h + 1) * MLSTM_DH] * (MLSTM_DH ** -0.5)
            vh = v_all[:, sl]
            i_col = gates[:, h:h + 1]
            f_col = gates[:, MLSTM_HEADS + h:MLSTM_HEADS + h + 1]
            i_row = to_row(i_col)
            f_row = to_row(f_col)
            m_prev = m_all[:, h:h + 1]
            c_prev = c_ref[b, h]
            n_prev = n_ref[b, h:h + 1, :]

            b_col = jnp.sum(jnp.where(causal, f_row, 0.0), axis=1, keepdims=True)
            b_row = to_row(b_col)
            dmat = jnp.where(causal, b_col - b_row + i_row, MASKED)
            a_col = b_col + m_prev
            mt = jnp.maximum(a_col, jnp.max(dmat, axis=1, keepdims=True))
            wa = jnp.exp(a_col - mt)
            decay = jnp.exp(dmat - mt)
            qkt = jnp.zeros((T, T), F32)
            for s in range(T):
                dots = jnp.sum(qh * kh[s:s + 1, :], axis=1, keepdims=True)
                qkt = jnp.where(s_idx == s, dots, qkt)
            sc = qkt * decay
            q8 = jnp.concatenate([qh, pad], axis=0).astype(BF16)
            inter = _dot(q8, c_prev.astype(BF16))[0:T, :]
            intra = jnp.zeros((T, MLSTM_DH), F32)
            for s in range(T):
                intra = intra + sc[:, s:s + 1] * vh[s:s + 1, :]
            num = wa * inter + intra
            den = wa * jnp.sum(qh * n_prev, axis=1, keepdims=True) + jnp.sum(sc, axis=1, keepdims=True)
            hh = num / jnp.maximum(jnp.abs(den), jnp.exp(-mt))

            bl = b_col[T - 1:T, :]
            g_col = bl - b_col + i_col
            m_new = jnp.maximum(bl + m_prev, jnp.max(g_col, axis=0, keepdims=True))
            wc = jnp.exp(bl + m_prev - m_new)
            kw = jnp.exp(g_col - m_new) * kh
            kw8 = jnp.concatenate([kw, pad], axis=0).astype(BF16)
            v8 = jnp.concatenate([vh, pad], axis=0).astype(BF16)
            c_out[b, h] = wc * c_prev + _dot_tn(kw8, v8)
            n_out[b, h:h + 1, :] = wc * n_prev + jnp.sum(kw, axis=0, keepdims=True)
            m_out[b, h:h + 1, :] = jnp.broadcast_to(m_new, (1, LANES))

            mu = jnp.mean(hh, axis=1, keepdims=True)
            var = jnp.mean(jnp.square(hh - mu), axis=1, keepdims=True)
            hn = (hh - mu) * lax.rsqrt(var + EPS) * gmh_ref[:, sl]
            hs.append(hn * jax.nn.sigmoid(om_all[:, sl]))
        hm_ref[b] = jnp.concatenate(hs, axis=1)
        return carry

    lax.fori_loop(0, group, one, 0, unroll=2)


def _mlstm_sample(z3, gates3, conv, c0, n0, m0, conv_w, conv_b, bg_row, g_mh, *, group):
    batch, steps, _ = z3.shape
    W2 = 2 * MLSTM_W
    full = lambda shape: pl.BlockSpec(shape, lambda g: (0,) * len(shape))
    return pl.pallas_call(
        functools.partial(_mlstm_sample_body, group=group, steps=steps),
        grid=(batch // group,),
        in_specs=[
            pl.BlockSpec((group, steps, W2), lambda g: (g, 0, COL_QK // W2)),
            pl.BlockSpec((group, steps, MLSTM_W), lambda g: (g, 0, COL_VM // MLSTM_W)),
            pl.BlockSpec((group, steps, MLSTM_W), lambda g: (g, 0, COL_OM // MLSTM_W)),
            pl.BlockSpec((group, steps, LANES), lambda g: (g, 0, 0)),
            pl.BlockSpec((group, CONV_W - 1, W2), lambda g: (g, 0, 0)),
            pl.BlockSpec((group, MLSTM_HEADS, MLSTM_DH, MLSTM_DH), lambda g: (g, 0, 0, 0)),
            pl.BlockSpec((group, MLSTM_HEADS, MLSTM_DH), lambda g: (g, 0, 0)),
            pl.BlockSpec((group, 1, LANES), lambda g: (g, 0, 0)),
            full((CONV_W, W2)), full((1, W2)), full((1, LANES)), full((1, MLSTM_W)),
        ],
        out_specs=[
            pl.BlockSpec((group, steps, MLSTM_W), lambda g: (g, 0, 0)),
            pl.BlockSpec((group, CONV_W - 1, W2), lambda g: (g, 0, 0)),
            pl.BlockSpec((group, MLSTM_HEADS, MLSTM_DH, MLSTM_DH), lambda g: (g, 0, 0, 0)),
            pl.BlockSpec((group, MLSTM_HEADS, MLSTM_DH), lambda g: (g, 0, 0)),
            pl.BlockSpec((group, MLSTM_HEADS, LANES), lambda g: (g, 0, 0)),
        ],
        out_shape=[
            jax.ShapeDtypeStruct((batch, steps, MLSTM_W), F32),
            jax.ShapeDtypeStruct((batch, CONV_W - 1, W2), F32),
            jax.ShapeDtypeStruct((batch, MLSTM_HEADS, MLSTM_DH, MLSTM_DH), F32),
            jax.ShapeDtypeStruct((batch, MLSTM_HEADS, MLSTM_DH), F32),
            jax.ShapeDtypeStruct((batch, MLSTM_HEADS, LANES), F32),
        ],
        compiler_params=_cparams(("arbitrary",), 32),
        name="mlstm_sample",
    )(z3, z3, z3, gates3, conv, c0, n0, m0, conv_w, conv_b, bg_row, g_mh)


def _branch_count(dist):
    cnt = jnp.zeros(dist.shape, F32)
    for window, d in DILATED:
        ok = (dist >= 0) & (dist <= window) & (lax.rem(jnp.maximum(dist, 0), d) == 0)
        cnt = cnt + jnp.where(ok, 1.0, 0.0)
    return cnt


def _dilated_sample_body(q_ref, kn_ref, vn_ref, kc_ref, vc_ref, o_ref, ot_s, fresh_s, den_s, *, steps, wb):
    Q = q_ref.shape[2]
    t_c = lax.broadcasted_iota(jnp.int32, (Q, wb), 0)
    pos_c = lax.broadcasted_iota(jnp.int32, (Q, wb), 1)
    cnt_c = _branch_count(wb + t_c - pos_c)
    t_n = lax.broadcasted_iota(jnp.int32, (Q, Q), 0)
    s_n = lax.broadcasted_iota(jnp.int32, (Q, Q), 1)
    cnt_n = jnp.where(s_n < steps, _branch_count(t_n - s_n), 0.0)
    lane_t = lax.broadcasted_iota(jnp.int32, (ATTN_DH, LANES), 1)

    def head(h, carry):
        qh = q_ref[0, h] * (ATTN_DH ** -0.5)
        knh = kn_ref[0, h]
        vnh = vn_ref[0, h]
        s_c = _dot(qh.astype(BF16), kc_ref[0, h].astype(BF16))
        s_new = jnp.zeros((Q, Q), F32)
        for s in range(steps):
            dots = jnp.sum(qh * knh[s:s + 1, :], axis=1, keepdims=True)
            s_new = jnp.where(s_n == s, dots, s_new)
        s_c = jnp.where(cnt_c > 0, s_c, MASKED)
        s_new = jnp.where(cnt_n > 0, s_new, MASKED)
        mx = jnp.maximum(jnp.max(s_c, axis=1, keepdims=True), jnp.max(s_new, axis=1, keepdims=True))
        p_c = cnt_c * jnp.exp(s_c - mx)
        p_n = cnt_n * jnp.exp(s_new - mx)
        den = jnp.sum(p_c, axis=1, keepdims=True) + jnp.sum(p_n, axis=1, keepdims=True)
        den_s[h] = jnp.broadcast_to(den, (Q, ATTN_DH))
        part = [jnp.zeros((ATTN_DH, LANES), F32) for _ in range(steps)]
        for c in range(wb // LANES):
            cs = slice(c * LANES, (c + 1) * LANES)
            v_c = vc_ref[0, h, :, cs]
            for t in range(steps):
                part[t] = part[t] + v_c * p_c[t:t + 1, cs]
        o_t = jnp.zeros((ATTN_DH, LANES), F32)
        for t in range(steps):
            o_t = jnp.where(lane_t == t, jnp.sum(part[t], axis=1, keepdims=True), o_t)
        ot_s[pl.ds(pl.multiple_of(h * ATTN_DH, ATTN_DH), ATTN_DH), :] = o_t
        o_new = jnp.zeros((Q, ATTN_DH), F32)
        for s in range(steps):
            o_new = o_new + p_n[:, s:s + 1] * vnh[s:s + 1, :]
        fresh_s[h] = o_new
        return carry

    lax.fori_loop(0, ATTN_HEADS, head, 0, unroll=ATTN_HEADS)
    o_cached = ot_s[...].T
    for h in range(ATTN_HEADS):
        o_ref[0, h] = (o_cached[0:Q, h * ATTN_DH:(h + 1) * ATTN_DH] + fresh_s[h]) / den_s[h]


def _dilated_sample(q8, kn8, vn8, kc_t, vc_t, *, steps):
    batch, _, qrows, _ = q8.shape
    wb = kc_t.shape[-1]
    new = lambda: pl.BlockSpec((1, ATTN_HEADS, qrows, ATTN_DH), lambda b: (b, 0, 0, 0))
    cache = lambda: pl.BlockSpec((1, ATTN_HEADS, ATTN_DH, wb), lambda b: (b, 0, 0, 0))
    return pl.pallas_call(
        functools.partial(_dilated_sample_body, steps=steps, wb=wb),
        grid=(batch,),
        in_specs=[new(), new(), new(), cache(), cache()],
        out_specs=new(),
        out_shape=jax.ShapeDtypeStruct((batch, ATTN_HEADS, qrows, ATTN_DH), F32),
        scratch_shapes=[pltpu.VMEM((ATTN_W, LANES), F32),
                        pltpu.VMEM((ATTN_HEADS, qrows, ATTN_DH), F32),
                        pltpu.VMEM((ATTN_HEADS, qrows, ATTN_DH), F32)],
        compiler_params=_cparams(("arbitrary",), 40),
        name="dilated_sample",
    )(q8, kn8, vn8, kc_t, vc_t)


def _xattn_sample_body(q_ref, k_ref, v_ref, o_ref):
    Q = q_ref.shape[1]
    mem_len = k_ref.shape[1]
    rows = mem_len * MEM_HEADS
    for g in range(q_ref.shape[0]):
        k2 = k_ref[g].reshape(rows, MEM_DH).astype(BF16)
        v2 = v_ref[g].reshape(rows, MEM_DH).astype(BF16)
        q_all = q_ref[g]
        qs = jnp.concatenate([q_all[:, h * MEM_DH:(h + 1) * MEM_DH] for h in range(MEM_HEADS)], axis=0)
        s = _dot_nt(qs, k2) * (MEM_DH ** -0.5)
        q_head = lax.div(lax.broadcasted_iota(jnp.int32, s.shape, 0), Q)
        k_head = lax.rem(lax.broadcasted_iota(jnp.int32, s.shape, 1), MEM_HEADS)
        s = jnp.where(q_head == k_head, s, MASKED)
        e = jnp.exp(s - jnp.max(s, axis=1, keepdims=True))
        p = e / jnp.sum(e, axis=1, keepdims=True)
        o = _dot(p.astype(BF16), v2)
        for h in range(MEM_HEADS):
            o_ref[g, :, h * MEM_DH:(h + 1) * MEM_DH] = o[h * Q:(h + 1) * Q, :].astype(BF16)


def _xattn_sample(q8, mem_k, mem_v, *, group):
    batch, qrows, d = q8.shape
    mem_len = mem_k.shape[1]
    mem = lambda: pl.BlockSpec((group, mem_len, MEM_HEADS, MEM_DH), lambda b: (b, 0, 0, 0))
    return pl.pallas_call(
        _xattn_sample_body,
        grid=(batch // group,),
        in_specs=[pl.BlockSpec((group, qrows, d), lambda b: (b, 0, 0)), mem(), mem()],
        out_specs=pl.BlockSpec((group, qrows, d), lambda b: (b, 0, 0)),
        out_shape=jax.ShapeDtypeStruct((batch, qrows, d), BF16),
        compiler_params=_cparams(("arbitrary",), 32),
        name="xattn_sample",
    )(q8, mem_k, mem_v)


def _pad_rows(a, rows):
    return jnp.pad(a, ((0, 0), (0, rows - a.shape[1]), (0, 0)))


def kernel(x_prompt, x_sample, mem_prompt, cache_attn_k, cache_attn_v, cache_mem_k, cache_mem_v, state_conv, state_C, state_n, state_m, g_mix, w_in, conv_w, conv_b, b_gates, g_mh, w_out, g_mem, w_mk, w_mv, g_xattn, w_mq, w_mo, g_ffn, w_gate, w_up, w_down, g_final):
    B, T, D = x_prompt.shape
    SB, ST, _ = x_sample.shape
    mem_len = mem_prompt.shape[1]
    W2 = 2 * MLSTM_W
    gate0 = COL_QA
    n_gate = 2 * MLSTM_HEADS

    w_main = jnp.concatenate([w_in[:, :gate0], w_in[:, gate0 + n_gate:]], axis=1).astype(BF16)
    w_gcols = jnp.pad(w_in[:, gate0:gate0 + n_gate], ((0, 0), (0, LANES - n_gate))).astype(BF16)
    bg_row = jnp.pad(b_gates, (0, LANES - n_gate)).reshape(1, LANES)
    w_mem = jnp.concatenate([w_mk, w_mv], axis=1).astype(BF16)
    w_out_m = w_out[:MLSTM_W].astype(BF16)
    w_out_a = w_out[MLSTM_W:].astype(BF16)
    w_mq_b, w_mo_b = w_mq.astype(BF16), w_mo.astype(BF16)
    w_gate_b, w_up_b, w_down_b = w_gate.astype(BF16), w_up.astype(BF16), w_down.astype(BF16)
    conv_b2 = conv_b.reshape(1, W2)
    g_mh2 = g_mh.reshape(1, MLSTM_W)

    xp = x_prompt.reshape(B * T, D)
    zmem = _norm_matmul(mem_prompt.reshape(B * mem_len, D), g_mem, w_mem, tm=B * mem_len, tn=512)
    z, gates = _norm_matmul(xp, g_mix, w_main, w_gcols, tm=2048, tn=512)
    hm, p_conv, p_C, p_n, p_m = _mlstm_prompt(z, gates, conv_w, conv_b2, bg_row, g_mh2,
                                              batch=B, seq=T, chunk=128)
    ha = _dilated_prompt(z, batch=B, seq=T, tq=2048)
    x1, q = _outproj_q(xp, hm, ha, w_out_m, w_out_a, g_xattn, w_mq_b, tm=512)
    o = _xattn_prompt(q, zmem, batch=B, seq=T, mem_len=mem_len, tm=512)
    y_prompt = _post_ffn(x1, o, w_mo_b, g_ffn, w_gate_b, w_up_b, w_down_b, g_final, tm=1024, tf=256)

    keep = min(DILATED[-1][0], T)
    z3p = z.reshape(B, T, N_MAIN)
    p_attn_k = z3p[:, T - keep:, COL_KA:COL_KA + ATTN_W].reshape(B, keep, ATTN_HEADS, ATTN_DH)
    p_attn_v = z3p[:, T - keep:, COL_VA:COL_VA + ATTN_W].reshape(B, keep, ATTN_HEADS, ATTN_DH)
    p_mem_k = zmem[:, :D].reshape(B, mem_len, MEM_HEADS, MEM_DH)
    p_mem_v = zmem[:, D:].reshape(B, mem_len, MEM_HEADS, MEM_DH)

    xs = x_sample.reshape(SB * ST, D)
    zs, gates_s = _norm_matmul(xs, g_mix, w_main, w_gcols, tm=SB * ST, tn=512)
    zs3 = zs.reshape(SB, ST, N_MAIN)
    m0 = jnp.pad(state_m, ((0, 0), (0, LANES - MLSTM_HEADS))).reshape(SB, 1, LANES)
    hm_s, s_conv, s_C, s_n, s_m = _mlstm_sample(
        zs3, gates_s.reshape(SB, ST, LANES), state_conv, state_C, state_n, m0,
        conv_w, conv_b2, bg_row, g_mh2, group=8)
    qrows = 8

    def heads_first(col0):
        a = zs3[:, :, col0:col0 + ATTN_W].reshape(SB, ST, ATTN_HEADS, ATTN_DH).transpose(0, 2, 1, 3)
        return jnp.pad(a, ((0, 0), (0, 0), (0, qrows - ST), (0, 0)))

    ha_s = _dilated_sample(heads_first(COL_QA), heads_first(COL_KA), heads_first(COL_VA),
                           jnp.transpose(cache_attn_k, (0, 2, 3, 1)),
                           jnp.transpose(cache_attn_v, (0, 2, 3, 1)), steps=ST)
    ha_s = ha_s[:, :, :ST].transpose(0, 2, 1, 3)
    x1s, qs = _outproj_q(xs, hm_s.reshape(SB * ST, MLSTM_W), ha_s.reshape(SB * ST, ATTN_W),
                         w_out_m, w_out_a, g_xattn, w_mq_b, tm=SB * ST)
    os_ = _xattn_sample(_pad_rows(qs.reshape(SB, ST, D), qrows), cache_mem_k, cache_mem_v, group=2)[:, :ST]
    y_sample = _post_ffn(x1s, os_.reshape(SB * ST, D), w_mo_b, g_ffn, w_gate_b, w_up_b, w_down_b,
                         g_final, tm=SB * ST, tf=256)

    s_attn_k = zs3[:, :, COL_KA:COL_KA + ATTN_W].reshape(SB, ST, ATTN_HEADS, ATTN_DH)
    s_attn_v = zs3[:, :, COL_VA:COL_VA + ATTN_W].reshape(SB, ST, ATTN_HEADS, ATTN_DH)

    return (y_prompt.reshape(B, T, D), y_sample.reshape(SB, ST, D), p_attn_k, p_attn_v,
            p_conv, p_C, p_n, p_m[:, :, 0], p_mem_k, p_mem_v,
            s_attn_k, s_attn_v, s_conv, s_C, s_n, s_m[:, :, 0])
```

```python
import functools

import jax
import jax.numpy as jnp
from jax import lax
from jax.experimental import pallas as pl
from jax.experimental.pallas import tpu as pltpu

F32 = jnp.float32
BF16 = jnp.bfloat16

EPS = 1e-6
MASKED = -1e30

MLSTM_HEADS = 4
MLSTM_DH = 128
MLSTM_W = MLSTM_HEADS * MLSTM_DH
ATTN_HEADS = 8
ATTN_DH = 64
ATTN_W = ATTN_HEADS * ATTN_DH
DILATED = ((128, 1), (512, 4), (2048, 16))
REL = 128
CONV_W = 4
MEM_HEADS = 4
MEM_DH = 256
LANES = 128
MIB = 1024 * 1024
PAD_PITCH = 24

COL_QK = 0
COL_VM = 2 * MLSTM_W
COL_OM = 3 * MLSTM_W
COL_QA = 4 * MLSTM_W
COL_KA = COL_QA + ATTN_W
COL_VA = COL_KA + ATTN_W
N_MAIN = COL_VA + ATTN_W


def _cparams(semantics, vmem_mib):
    return pltpu.CompilerParams(dimension_semantics=semantics, vmem_limit_bytes=vmem_mib * MIB)


def _rms(x, g):
    return x * lax.rsqrt(jnp.mean(x * x, axis=-1, keepdims=True) + EPS) * g


def _log_sigmoid(x):
    return jnp.minimum(x, 0.0) - jnp.log1p(jnp.exp(-jnp.abs(x)))


def _dot(a, b):
    return jnp.dot(a, b, preferred_element_type=F32)


def _dot_nt(a, b):
    return lax.dot_general(a, b, (((1,), (1,)), ((), ())), preferred_element_type=F32)


def _dot_tn(a, b):
    return lax.dot_general(a, b, (((0,), (0,)), ((), ())), preferred_element_type=F32)


def _norm_matmul_body(x_ref, g_ref, w_ref, *rest, with_side):
    if with_side:
        ws_ref, z_ref, side_ref, h_ref = rest
    else:
        z_ref, h_ref = rest

    @pl.when(pl.program_id(1) == 0)
    def _():
        h = _rms(x_ref[...], g_ref[...]).astype(BF16)
        h_ref[...] = h
        if with_side:
            side_ref[...] = _dot(h, ws_ref[...])

    z_ref[...] = _dot(h_ref[...], w_ref[...])


def _norm_matmul(x, g, w, w_side=None, *, tm, tn):
    n, d = x.shape
    c = w.shape[1]
    with_side = w_side is not None
    in_specs = [
        pl.BlockSpec((tm, d), lambda i, j: (i, 0)),
        pl.BlockSpec((1, d), lambda i, j: (0, 0)),
        pl.BlockSpec((d, tn), lambda i, j: (0, j)),
    ]
    out_specs = [pl.BlockSpec((tm, tn), lambda i, j: (i, j))]
    out_shape = [jax.ShapeDtypeStruct((n, c), F32)]
    args = [x, g.reshape(1, d), w]
    if with_side:
        in_specs.append(pl.BlockSpec((d, LANES), lambda i, j: (0, 0)))
        out_specs.append(pl.BlockSpec((tm, LANES), lambda i, j: (i, 0)))
        out_shape.append(jax.ShapeDtypeStruct((n, LANES), F32))
        args.append(w_side)
    vmem = 2 * tm * d * 4 + tm * d * 2 + 2 * d * tn * 2 + 2 * tm * tn * 4 + 4 * tm * LANES * 4
    res = pl.pallas_call(
        functools.partial(_norm_matmul_body, with_side=with_side),
        grid=(n // tm, c // tn),
        in_specs=in_specs,
        out_specs=out_specs,
        out_shape=out_shape,
        scratch_shapes=[pltpu.VMEM((tm, d), BF16)],
        compiler_params=_cparams(("arbitrary", "arbitrary"), vmem // MIB + 8),
        name="norm_matmul",
    )(*args)
    return res if with_side else res[0]


def _mlstm_prompt_body(qk_ref, v_ref, om_ref, gt_ref, cw_ref, cb_ref, bg_ref, gmh_ref,
                       hm_ref, conv_ref, c_out, n_out, m_out,
                       ubuf, c_s, n_s, m_s, *, chunk, nchunk):
    L = chunk
    H = MLSTM_HEADS
    rows_step = L * nchunk
    c = pl.program_id(1)
    last = pl.num_programs(1) - 1
    W2 = 2 * MLSTM_W

    @pl.when(c == 0)
    def _():
        ubuf[0:8, :] = jnp.zeros((8, W2), F32)
        c_s[...] = jnp.zeros_like(c_s)
        n_s[...] = jnp.zeros_like(n_s)
        m_s[...] = jnp.zeros_like(m_s)

    @pl.when(c > 0)
    def _():
        ubuf[0:8, :] = ubuf[rows_step:rows_step + 8, :]

    ubuf[8:rows_step + 8, :] = qk_ref[...]

    t_idx = lax.broadcasted_iota(jnp.int32, (L, L), 0)
    s_idx = lax.broadcasted_iota(jnp.int32, (L, L), 1)
    causal = s_idx <= t_idx
    tril = jnp.where(causal, 1.0, 0.0).astype(BF16)
    col = lax.broadcasted_iota(jnp.int32, (L, LANES), 1)

    units = [(ci, h) for ci in range(nchunk) for h in range(H)]
    pre = []
    for ci in range(nchunk):
        r0 = ci * L
        y = jnp.broadcast_to(cb_ref[...], (L, W2))
        for j in range(CONV_W):
            lo = r0 + 8 - (CONV_W - 1) + j
            y = y + ubuf[lo:lo + L, :] * cw_ref[j:j + 1, :]
        qk = y * jax.nn.sigmoid(y)
        gates = gt_ref[r0:r0 + L, :] + bg_ref[...]
        gates = jnp.where(col >= H, _log_sigmoid(gates), gates)
        g_hi = gates.astype(BF16)
        g_lo = (gates - g_hi.astype(F32)).astype(BF16)
        csum = _dot(tril, g_hi) + _dot(tril, g_lo)
        pre.append((qk, gates, csum, gates.T, csum.T))

    def unit(ci, h):
        qk, gates, csum, gates_t, csum_t = pre[ci]
        sl = slice(h * MLSTM_DH, (h + 1) * MLSTM_DH)
        qh = qk[:, sl]
        kh = qk[:, MLSTM_W + h * MLSTM_DH:MLSTM_W + (h + 1) * MLSTM_DH] * (MLSTM_DH ** -0.5)
        b_row = csum_t[H + h:H + h + 1, :]
        return dict(qh=qh, kh=kh, qb=qh.astype(BF16), kb=kh.astype(BF16),
                    vb=v_ref[ci * L:(ci + 1) * L, sl].astype(BF16),
                    i_col=gates[:, h:h + 1], i_row=gates_t[h:h + 1, :],
                    b_col=csum[:, H + h:H + h + 1], b_row=b_row, bl=b_row[:, L - 1:L])

    U = {u: unit(*u) for u in units}
    for d in U.values():
        d['dmat'] = jnp.where(causal, d['b_col'] - d['b_row'] + d['i_row'], MASKED)
        d['qk'] = _dot_nt(d['qb'], d['kb'])
        d['g_col'] = d['bl'] - d['b_col'] + d['i_col']
    for d in U.values():
        d['m_loc'] = jnp.max(d['dmat'], axis=1, keepdims=True)
        d['g_max'] = jnp.max(d['g_col'], axis=0, keepdims=True)
    for d in U.values():
        d['sc'] = d['qk'] * jnp.exp(d['dmat'] - d['m_loc'])
        d['kw'] = jnp.exp(d['g_col'] - d['g_max']) * d['kh']
    for d in U.values():
        d['a_loc'] = _dot(d['sc'].astype(BF16), d['vb'])
        d['r_loc'] = jnp.sum(d['sc'], axis=1, keepdims=True)
        d['dc_loc'] = _dot_tn(d['kw'].astype(BF16), d['vb'])
        d['dn_loc'] = jnp.sum(d['kw'], axis=0, keepdims=True)

    c_st = [c_s[h] for h in range(H)]
    n_st = [n_s[h:h + 1, :] for h in range(H)]
    m_st = [m_s[h:h + 1, 0:1] for h in range(H)]
    for (ci, h), d in U.items():
        m_prev, c_prev, n_prev = m_st[h], c_st[h], n_st[h]
        a_col = d['b_col'] + m_prev
        mt = jnp.maximum(a_col, d['m_loc'])
        wa = jnp.exp(a_col - mt)
        wl = jnp.exp(d['m_loc'] - mt)
        num = wa * _dot(d['qb'], c_prev.astype(BF16)) + wl * d['a_loc']
        den = wa * jnp.sum(d['qh'] * n_prev, axis=1, keepdims=True) + wl * d['r_loc']
        d['hh'] = num / jnp.maximum(jnp.abs(den), jnp.exp(-mt))
        m_new = jnp.maximum(d['bl'] + m_prev, d['g_max'])
        wc = jnp.exp(d['bl'] + m_prev - m_new)
        wg = jnp.exp(d['g_max'] - m_new)
        c_st[h] = wc * c_prev + wg * d['dc_loc']
        n_st[h] = wc * n_prev + wg * d['dn_loc']
        m_st[h] = m_new
    for h in range(H):
        c_s[h] = c_st[h]
        n_s[h:h + 1, :] = n_st[h]
        m_s[h:h + 1, :] = jnp.broadcast_to(m_st[h], (1, LANES))

    for (ci, h), d in U.items():
        sl = slice(h * MLSTM_DH, (h + 1) * MLSTM_DH)
        hh = d['hh']
        mu = jnp.mean(hh, axis=1, keepdims=True)
        var = jnp.mean(jnp.square(hh - mu), axis=1, keepdims=True)
        hn = (hh - mu) * lax.rsqrt(var + EPS) * gmh_ref[:, sl]
        hm_ref[ci * L:(ci + 1) * L, sl] = hn * jax.nn.sigmoid(om_ref[ci * L:(ci + 1) * L, sl])

    @pl.when(c == last)
    def _():
        conv_ref[0] = ubuf[rows_step + 8 - (CONV_W - 1):rows_step + 8, :]
        c_out[0] = c_s[...]
        n_out[0] = n_s[...]
        m_out[0] = m_s[...]


def _mlstm_prompt(z, gates, conv_w, conv_b, bg_row, g_mh, *, batch, seq, chunk, nchunk):
    rows = chunk * nchunk
    nc = seq // rows
    n = batch * seq
    row = lambda b, c: b * nc + c
    W2 = 2 * MLSTM_W
    full = lambda shape: pl.BlockSpec(shape, lambda b, c: (0,) * len(shape))
    return pl.pallas_call(
        functools.partial(_mlstm_prompt_body, chunk=chunk, nchunk=nchunk),
        grid=(batch, nc),
        in_specs=[
            pl.BlockSpec((rows, W2), lambda b, c: (row(b, c), COL_QK // W2)),
            pl.BlockSpec((rows, MLSTM_W), lambda b, c: (row(b, c), COL_VM // MLSTM_W)),
            pl.BlockSpec((rows, MLSTM_W), lambda b, c: (row(b, c), COL_OM // MLSTM_W)),
            pl.BlockSpec((rows, LANES), lambda b, c: (row(b, c), 0)),
            full((CONV_W, W2)), full((1, W2)), full((1, LANES)), full((1, MLSTM_W)),
        ],
        out_specs=[
            pl.BlockSpec((rows, MLSTM_W), lambda b, c: (row(b, c), 0)),
            pl.BlockSpec((1, CONV_W - 1, W2), lambda b, c: (b, 0, 0)),
            pl.BlockSpec((1, MLSTM_HEADS, MLSTM_DH, MLSTM_DH), lambda b, c: (b, 0, 0, 0)),
            pl.BlockSpec((1, MLSTM_HEADS, MLSTM_DH), lambda b, c: (b, 0, 0)),
            pl.BlockSpec((1, MLSTM_HEADS, LANES), lambda b, c: (b, 0, 0)),
        ],
        out_shape=[
            jax.ShapeDtypeStruct((n, MLSTM_W), F32),
            jax.ShapeDtypeStruct((batch, CONV_W - 1, W2), F32),
            jax.ShapeDtypeStruct((batch, MLSTM_HEADS, MLSTM_DH, MLSTM_DH), F32),
            jax.ShapeDtypeStruct((batch, MLSTM_HEADS, MLSTM_DH), F32),
            jax.ShapeDtypeStruct((batch, MLSTM_HEADS, LANES), F32),
        ],
        scratch_shapes=[
            pltpu.VMEM((rows + 8, W2), F32),
            pltpu.VMEM((MLSTM_HEADS, MLSTM_DH, MLSTM_DH), F32),
            pltpu.VMEM((MLSTM_HEADS, MLSTM_DH), F32),
            pltpu.VMEM((MLSTM_HEADS, LANES), F32),
        ],
        compiler_params=_cparams(("arbitrary", "arbitrary"), 32),
        name="mlstm_prompt",
    )(z, z, z, gates, conv_w, conv_b, bg_row, g_mh)


def _dilated_prompt_body(q_ref, k_ref, kp_ref, v_ref, vp_ref, o_ref, kpad, vpad, acc, mm, ll, bias, *, tq):
    i = pl.program_id(2)
    hd = ATTN_DH
    dmax = max(d for _, d in DILATED)
    ngrp = tq // dmax
    half = ngrp * PAD_PITCH
    cur = lax.rem(i, 2) * half
    prev = lax.rem(i + 1, 2) * half

    @pl.when(i == 0)
    def _():
        kpad[pl.ds(pl.multiple_of(prev, 8), half), :] = jnp.zeros((half, LANES), F32)
        vpad[pl.ds(pl.multiple_of(prev, 8), half), :] = jnp.zeros((half, LANES), F32)

    def pad_copy(g, carry):
        src = pl.ds(pl.multiple_of(g * dmax, dmax), dmax)
        dst = pl.ds(pl.multiple_of(cur + g * PAD_PITCH, 8), dmax)
        kpad[dst, :] = k_ref[src, :]
        vpad[dst, :] = v_ref[src, :]
        return carry

    lax.fori_loop(0, ngrp, pad_copy, 0, unroll=8)

    lane = lax.broadcasted_iota(jnp.int32, (REL, LANES), 1)
    first_head = lane < hd
    a_idx = lax.broadcasted_iota(jnp.int32, (REL, 2 * REL), 0)
    j_idx = lax.broadcasted_iota(jnp.int32, (REL, 2 * REL), 1)
    band = (j_idx >= a_idx) & (j_idx <= a_idx + REL)
    b0 = jnp.where(band, 0.0, MASKED)
    b1 = jnp.where(band & (j_idx >= jnp.where(i == 0, REL, 0)), 0.0, MASKED)
    bias[0] = jnp.concatenate([b0, b0], axis=0)
    bias[1] = jnp.concatenate([b1, b1], axis=0)

    order = sorted(DILATED, key=lambda wd: -wd[1])
    for bi, (window, d) in enumerate(order):
        assert window // d == REL and tq % (REL * d) == 0
        nblk = tq // (REL * d)
        for r in range(d):
            for cb in range(nblk):
                rows = pl.ds(r + d * REL * cb, REL, stride=d)
                q2 = q_ref[rows, :] * (hd ** -0.5)
                if d == dmax:
                    kp = kpad[pl.ds(prev + r, REL, stride=PAD_PITCH), :]
                    kc = kpad[pl.ds(cur + r, REL, stride=PAD_PITCH), :]
                    vp = vpad[pl.ds(prev + r, REL, stride=PAD_PITCH), :]
                    vc = vpad[pl.ds(cur + r, REL, stride=PAD_PITCH), :]
                    k2 = jnp.concatenate([kp, kc], axis=0)
                    v2 = jnp.concatenate([vp, vc], axis=0)
                elif cb == 0:
                    tail = pl.ds(r + d * REL * (nblk - 1), REL, stride=d)
                    k2 = jnp.concatenate([kp_ref[tail, :], k_ref[rows, :]], axis=0)
                    v2 = jnp.concatenate([vp_ref[tail, :], v_ref[rows, :]], axis=0)
                else:
                    both = pl.ds(r + d * REL * (cb - 1), 2 * REL, stride=d)
                    k2 = k_ref[both, :]
                    v2 = v_ref[both, :]
                qs = jnp.concatenate([jnp.where(first_head, q2, 0.0), jnp.where(first_head, 0.0, q2)], axis=0)
                s = _dot_nt(qs.astype(BF16), k2.astype(BF16)) + bias[1 if cb == 0 else 0]
                mx = jnp.max(s, axis=1, keepdims=True)
                p = jnp.exp(s - mx)
                l = jnp.sum(p, axis=1, keepdims=True)
                o = _dot(p.astype(BF16), v2.astype(BF16))
                m_blk = jnp.where(first_head, mx[0:REL], mx[REL:2 * REL])
                l_blk = jnp.where(first_head, l[0:REL], l[REL:2 * REL])
                o_blk = jnp.where(first_head, o[0:REL], o[REL:2 * REL])
                if bi == 0:
                    mm[rows, :] = m_blk
                    ll[rows, :] = l_blk
                    acc[rows, :] = o_blk
                else:
                    m_old = mm[rows, :]
                    m_new = jnp.maximum(m_old, m_blk)
                    w_old = jnp.exp(m_old - m_new)
                    w_blk = jnp.exp(m_blk - m_new)
                    mm[rows, :] = m_new
                    ll[rows, :] = ll[rows, :] * w_old + l_blk * w_blk
                    acc[rows, :] = acc[rows, :] * w_old + o_blk * w_blk

    o_ref[...] = acc[...] / ll[...]


def _dilated_prompt(z, *, batch, seq, tq):
    nt = seq // tq
    n = batch * seq
    pairs = ATTN_W // LANES
    dmax = max(d for _, d in DILATED)
    assert tq == REL * dmax
    pad_rows = 2 * (tq // dmax) * PAD_PITCH

    def spec(col0, back):
        return pl.BlockSpec((tq, LANES),
                            lambda b, p, i: (b * nt + jnp.maximum(i - back, 0), col0 // LANES + p))

    return pl.pallas_call(
        functools.partial(_dilated_prompt_body, tq=tq),
        grid=(batch, pairs, nt),
        in_specs=[spec(COL_QA, 0), spec(COL_KA, 0), spec(COL_KA, 1), spec(COL_VA, 0), spec(COL_VA, 1)],
        out_specs=pl.BlockSpec((tq, LANES), lambda b, p, i: (b * nt + i, p)),
        out_shape=jax.ShapeDtypeStruct((n, ATTN_W), F32),
        scratch_shapes=[
            pltpu.VMEM((pad_rows, LANES), F32), pltpu.VMEM((pad_rows, LANES), F32),
            pltpu.VMEM((tq, LANES), F32), pltpu.VMEM((tq, LANES), F32), pltpu.VMEM((tq, LANES), F32),
            pltpu.VMEM((2, 2 * REL, 2 * REL), F32),
        ],
        compiler_params=_cparams(("arbitrary", "arbitrary", "arbitrary"), 32),
        name="dilated_prompt",
    )(z, z, z, z, z)


def _outproj_q_body(x_ref, hm_ref, ha_ref, wom_ref, woa_ref, gx_ref, wq_ref, x1_ref, q_ref):
    x1 = x_ref[...] + _dot(hm_ref[...].astype(BF16), wom_ref[...]) + _dot(ha_ref[...].astype(BF16), woa_ref[...])
    x1_ref[...] = x1
    q_ref[...] = _dot(_rms(x1, gx_ref[...]).astype(BF16), wq_ref[...]).astype(BF16)


def _outproj_q(x, hm, ha, w_out_m, w_out_a, g_x, w_mq, *, tm):
    n, d = x.shape
    row = lambda w: pl.BlockSpec((tm, w), lambda i: (i, 0))
    full = lambda shape: pl.BlockSpec(shape, lambda i: (0, 0))
    return pl.pallas_call(
        _outproj_q_body,
        grid=(n // tm,),
        in_specs=[row(d), row(MLSTM_W), row(ATTN_W), full((MLSTM_W, d)), full((ATTN_W, d)),
                  full((1, d)), full((d, d))],
        out_specs=[row(d), row(d)],
        out_shape=[jax.ShapeDtypeStruct((n, d), F32), jax.ShapeDtypeStruct((n, d), BF16)],
        compiler_params=_cparams(("arbitrary",), 40),
        name="outproj_q",
    )(x, hm, ha, w_out_m, w_out_a, g_x.reshape(1, d), w_mq)


def _xattn_prompt_body(q_ref, k_ref, v_ref, o_ref):
    for h in range(MEM_HEADS):
        sl = slice(h * MEM_DH, (h + 1) * MEM_DH)
        s = _dot_nt(q_ref[:, sl], k_ref[:, sl].astype(BF16)) * (MEM_DH ** -0.5)
        e = jnp.exp(s - jnp.max(s, axis=1, keepdims=True))
        p = e / jnp.sum(e, axis=1, keepdims=True)
        o_ref[:, sl] = _dot(p.astype(BF16), v_ref[:, sl].astype(BF16)).astype(BF16)


def _xattn_prompt(q, zmem, *, batch, seq, mem_len, tm):
    n, d = q.shape
    nt = seq // tm
    return pl.pallas_call(
        _xattn_prompt_body,
        grid=(batch, nt),
        in_specs=[
            pl.BlockSpec((tm, d), lambda b, i: (b * nt + i, 0)),
            pl.BlockSpec((mem_len, d), lambda b, i: (b, 0)),
            pl.BlockSpec((mem_len, d), lambda b, i: (b, 1)),
        ],
        out_specs=pl.BlockSpec((tm, d), lambda b, i: (b * nt + i, 0)),
        out_shape=jax.ShapeDtypeStruct((n, d), BF16),
        compiler_params=_cparams(("arbitrary", "arbitrary"), 32),
        name="xattn_prompt",
    )(q, zmem, zmem)


def _post_ffn_body(x1_ref, o_ref, wmo_ref, gf_ref, wg_ref, wu_ref, wd_ref, gfin_ref, y_ref):
    x2 = x1_ref[...] + _dot(o_ref[...], wmo_ref[...])
    h = _rms(x2, gf_ref[...]).astype(BF16)
    g = _dot(h, wg_ref[...])
    u = _dot(h, wu_ref[...])
    x3 = x2 + _dot((g * jax.nn.sigmoid(g) * u).astype(BF16), wd_ref[...])
    y_ref[...] = _rms(x3, gfin_ref[...])


def _post_ffn(x1, o, w_mo, g_ffn, w_gate, w_up, w_down, g_final, *, tm):
    n, d = x1.shape
    dff = w_gate.shape[1]
    tm = min(tm, n)
    assert n % tm == 0
    row = lambda: pl.BlockSpec((tm, d), lambda i: (i, 0))
    full = lambda shape: pl.BlockSpec(shape, lambda i: (0, 0), pipeline_mode=pl.Buffered(1))
    weights = 2 * (d * d + 3 * d * dff)
    work = tm * (6 * d * 4 + 2 * dff * 4 + dff * 2 + d * 2)
    return pl.pallas_call(
        _post_ffn_body,
        grid=(n // tm,),
        in_specs=[row(), row(), full((d, d)), full((1, d)), full((d, dff)), full((d, dff)),
                  full((dff, d)), full((1, d))],
        out_specs=row(),
        out_shape=jax.ShapeDtypeStruct((n, d), F32),
        compiler_params=_cparams(("arbitrary",), (weights + work) // MIB + 8),
        name="post_ffn",
    )(x1, o, w_mo, g_ffn.reshape(1, d), w_gate, w_up, w_down, g_final.reshape(1, d))


def _mlstm_sample_body(qk_ref, v_ref, om_ref, gt_ref, conv_ref, c_ref, n_ref, m_ref,
                       cw_ref, cb_ref, bg_ref, gmh_ref,
                       hm_ref, conv_out, c_out, n_out, m_out, *, group, steps):
    T = steps
    W2 = 2 * MLSTM_W
    t_idx = lax.broadcasted_iota(jnp.int32, (T, T), 0)
    s_idx = lax.broadcasted_iota(jnp.int32, (T, T), 1)
    causal = s_idx <= t_idx
    eye = s_idx == t_idx
    col = lax.broadcasted_iota(jnp.int32, (T, LANES), 1)
    pad = jnp.zeros((8 - T, MLSTM_DH), F32)

    def to_row(v_col):
        return jnp.sum(jnp.where(eye, v_col, 0.0), axis=0, keepdims=True)

    def one(b, carry):
        up = jnp.concatenate([conv_ref[b], qk_ref[b]], axis=0)
        y = jnp.broadcast_to(cb_ref[...], (T, W2))
        for j in range(CONV_W):
            y = y + up[j:j + T, :] * cw_ref[j:j + 1, :]
        conv_out[b] = up[T:T + CONV_W - 1, :]
        qk = y * jax.nn.sigmoid(y)
        gates = gt_ref[b] + bg_ref[...]
        gates = jnp.where(col >= MLSTM_HEADS, _log_sigmoid(gates), gates)
        v_all = v_ref[b]
        om_all = om_ref[b]
        m_all = m_ref[b]
        hs = []
        for h in range(MLSTM_HEADS):
            sl = slice(h * MLSTM_DH, (h + 1) * MLSTM_DH)
            qh = qk[:, sl]
            kh = qk[:, MLSTM_W + h * MLSTM_DH:MLSTM_W + (h + 1) * MLSTM_DH] * (MLSTM_DH ** -0.5)
            vh = v_all[:, sl]
            i_col = gates[:, h:h + 1]
            f_col = gates[:, MLSTM_HEADS + h:MLSTM_HEADS + h + 1]
            i_row = to_row(i_col)
            f_row = to_row(f_col)
            m_prev = m_all[:, h:h + 1]
            c_prev = c_ref[b, h]
            n_prev = n_ref[b, h:h + 1, :]

            b_col = jnp.sum(jnp.where(causal, f_row, 0.0), axis=1, keepdims=True)
            b_row = to_row(b_col)
            dmat = jnp.where(causal, b_col - b_row + i_row, MASKED)
            a_col = b_col + m_prev
            mt = jnp.maximum(a_col, jnp.max(dmat, axis=1, keepdims=True))
            wa = jnp.exp(a_col - mt)
            decay = jnp.exp(dmat - mt)
            qkt = jnp.zeros((T, T), F32)
            for s in range(T):
                dots = jnp.sum(qh * kh[s:s + 1, :], axis=1, keepdims=True)
                qkt = jnp.where(s_idx == s, dots, qkt)
            sc = qkt * decay
            q8 = jnp.concatenate([qh, pad], axis=0).astype(BF16)
            inter = _dot(q8, c_prev.astype(BF16))[0:T, :]
            intra = jnp.zeros((T, MLSTM_DH), F32)
            for s in range(T):
                intra = intra + sc[:, s:s + 1] * vh[s:s + 1, :]
            num = wa * inter + intra
            den = wa * jnp.sum(qh * n_prev, axis=1, keepdims=True) + jnp.sum(sc, axis=1, keepdims=True)
            hh = num / jnp.maximum(jnp.abs(den), jnp.exp(-mt))

            bl = b_col[T - 1:T, :]
            g_col = bl - b_col + i_col
            m_new = jnp.maximum(bl + m_prev, jnp.max(g_col, axis=0, keepdims=True))
            wc = jnp.exp(bl + m_prev - m_new)
            kw = jnp.exp(g_col - m_new) * kh
            kw8 = jnp.concatenate([kw, pad], axis=0).astype(BF16)
            v8 = jnp.concatenate([vh, pad], axis=0).astype(BF16)
            c_out[b, h] = wc * c_prev + _dot_tn(kw8, v8)
            n_out[b, h:h + 1, :] = wc * n_prev + jnp.sum(kw, axis=0, keepdims=True)
            m_out[b, h:h + 1, :] = jnp.broadcast_to(m_new, (1, LANES))

            mu = jnp.mean(hh, axis=1, keepdims=True)
            var = jnp.mean(jnp.square(hh - mu), axis=1, keepdims=True)
            hn = (hh - mu) * lax.rsqrt(var + EPS) * gmh_ref[:, sl]
            hs.append(hn * jax.nn.sigmoid(om_all[:, sl]))
        hm_ref[b] = jnp.concatenate(hs, axis=1)
        return carry

    lax.fori_loop(0, group, one, 0, unroll=2)


def _mlstm_sample(z3, gates3, conv, c0, n0, m0, conv_w, conv_b, bg_row, g_mh, *, group):
    batch, steps, _ = z3.shape
    W2 = 2 * MLSTM_W
    full = lambda shape: pl.BlockSpec(shape, lambda g: (0,) * len(shape))
    return pl.pallas_call(
        functools.partial(_mlstm_sample_body, group=group, steps=steps),
        grid=(batch // group,),
        in_specs=[
            pl.BlockSpec((group, steps, W2), lambda g: (g, 0, COL_QK // W2)),
            pl.BlockSpec((group, steps, MLSTM_W), lambda g: (g, 0, COL_VM // MLSTM_W)),
            pl.BlockSpec((group, steps, MLSTM_W), lambda g: (g, 0, COL_OM // MLSTM_W)),
            pl.BlockSpec((group, steps, LANES), lambda g: (g, 0, 0)),
            pl.BlockSpec((group, CONV_W - 1, W2), lambda g: (g, 0, 0)),
            pl.BlockSpec((group, MLSTM_HEADS, MLSTM_DH, MLSTM_DH), lambda g: (g, 0, 0, 0)),
            pl.BlockSpec((group, MLSTM_HEADS, MLSTM_DH), lambda g: (g, 0, 0)),
            pl.BlockSpec((group, 1, LANES), lambda g: (g, 0, 0)),
            full((CONV_W, W2)), full((1, W2)), full((1, LANES)), full((1, MLSTM_W)),
        ],
        out_specs=[
            pl.BlockSpec((group, steps, MLSTM_W), lambda g: (g, 0, 0)),
            pl.BlockSpec((group, CONV_W - 1, W2), lambda g: (g, 0, 0)),
            pl.BlockSpec((group, MLSTM_HEADS, MLSTM_DH, MLSTM_DH), lambda g: (g, 0, 0, 0)),
            pl.BlockSpec((group, MLSTM_HEADS, MLSTM_DH), lambda g: (g, 0, 0)),
            pl.BlockSpec((group, MLSTM_HEADS, LANES), lambda g: (g, 0, 0)),
        ],
        out_shape=[
            jax.ShapeDtypeStruct((batch, steps, MLSTM_W), F32),
            jax.ShapeDtypeStruct((batch, CONV_W - 1, W2), F32),
            jax.ShapeDtypeStruct((batch, MLSTM_HEADS, MLSTM_DH, MLSTM_DH), F32),
            jax.ShapeDtypeStruct((batch, MLSTM_HEADS, MLSTM_DH), F32),
            jax.ShapeDtypeStruct((batch, MLSTM_HEADS, LANES), F32),
        ],
        compiler_params=_cparams(("arbitrary",), 32),
        name="mlstm_sample",
    )(z3, z3, z3, gates3, conv, c0, n0, m0, conv_w, conv_b, bg_row, g_mh)


def _branch_count(dist):
    cnt = jnp.zeros(dist.shape, F32)
    for window, d in DILATED:
        ok = (dist >= 0) & (dist <= window) & (lax.rem(jnp.maximum(dist, 0), d) == 0)
        cnt = cnt + jnp.where(ok, 1.0, 0.0)
    return cnt


def _dilated_sample_body(q_ref, kn_ref, vn_ref, kc_ref, vc_ref, o_ref, ot_s, fresh_s, den_s, *, steps, wb):
    Q = q_ref.shape[2]
    t_c = lax.broadcasted_iota(jnp.int32, (Q, wb), 0)
    pos_c = lax.broadcasted_iota(jnp.int32, (Q, wb), 1)
    cnt_c = _branch_count(wb + t_c - pos_c)
    t_n = lax.broadcasted_iota(jnp.int32, (Q, Q), 0)
    s_n = lax.broadcasted_iota(jnp.int32, (Q, Q), 1)
    cnt_n = jnp.where(s_n < steps, _branch_count(t_n - s_n), 0.0)
    lane_t = lax.broadcasted_iota(jnp.int32, (ATTN_DH, LANES), 1)

    def head(h, carry):
        qh = q_ref[0, h] * (ATTN_DH ** -0.5)
        knh = kn_ref[0, h]
        vnh = vn_ref[0, h]
        s_c = _dot(qh.astype(BF16), kc_ref[0, h].astype(BF16))
        s_new = jnp.zeros((Q, Q), F32)
        for s in range(steps):
            dots = jnp.sum(qh * knh[s:s + 1, :], axis=1, keepdims=True)
            s_new = jnp.where(s_n == s, dots, s_new)
        s_c = jnp.where(cnt_c > 0, s_c, MASKED)
        s_new = jnp.where(cnt_n > 0, s_new, MASKED)
        mx = jnp.maximum(jnp.max(s_c, axis=1, keepdims=True), jnp.max(s_new, axis=1, keepdims=True))
        p_c = cnt_c * jnp.exp(s_c - mx)
        p_n = cnt_n * jnp.exp(s_new - mx)
        den = jnp.sum(p_c, axis=1, keepdims=True) + jnp.sum(p_n, axis=1, keepdims=True)
        den_s[h] = jnp.broadcast_to(den, (Q, ATTN_DH))
        part = [jnp.zeros((ATTN_DH, LANES), F32) for _ in range(steps)]
        for c in range(wb // LANES):
            cs = slice(c * LANES, (c + 1) * LANES)
            v_c = vc_ref[0, h, :, cs]
            for t in range(steps):
                part[t] = part[t] + v_c * p_c[t:t + 1, cs]
        o_t = jnp.zeros((ATTN_DH, LANES), F32)
        for t in range(steps):
            o_t = jnp.where(lane_t == t, jnp.sum(part[t], axis=1, keepdims=True), o_t)
        ot_s[pl.ds(pl.multiple_of(h * ATTN_DH, ATTN_DH), ATTN_DH), :] = o_t
        o_new = jnp.zeros((Q, ATTN_DH), F32)
        for s in range(steps):
            o_new = o_new + p_n[:, s:s + 1] * vnh[s:s + 1, :]
        fresh_s[h] = o_new
        return carry

    lax.fori_loop(0, ATTN_HEADS, head, 0, unroll=ATTN_HEADS)
    o_cached = ot_s[...].T
    for h in range(ATTN_HEADS):
        o_ref[0, h] = (o_cached[0:Q, h * ATTN_DH:(h + 1) * ATTN_DH] + fresh_s[h]) / den_s[h]


def _dilated_sample(q8, kn8, vn8, kc_t, vc_t, *, steps):
    batch, _, qrows, _ = q8.shape
    wb = kc_t.shape[-1]
    new = lambda: pl.BlockSpec((1, ATTN_HEADS, qrows, ATTN_DH), lambda b: (b, 0, 0, 0))
    cache = lambda: pl.BlockSpec((1, ATTN_HEADS, ATTN_DH, wb), lambda b: (b, 0, 0, 0))
    return pl.pallas_call(
        functools.partial(_dilated_sample_body, steps=steps, wb=wb),
        grid=(batch,),
        in_specs=[new(), new(), new(), cache(), cache()],
        out_specs=new(),
        out_shape=jax.ShapeDtypeStruct((batch, ATTN_HEADS, qrows, ATTN_DH), F32),
        scratch_shapes=[pltpu.VMEM((ATTN_W, LANES), F32),
                        pltpu.VMEM((ATTN_HEADS, qrows, ATTN_DH), F32),
                        pltpu.VMEM((ATTN_HEADS, qrows, ATTN_DH), F32)],
        compiler_params=_cparams(("arbitrary",), 40),
        name="dilated_sample",
    )(q8, kn8, vn8, kc_t, vc_t)


def _xattn_sample_body(q_ref, k_ref, v_ref, o_ref):
    Q = q_ref.shape[1]
    mem_len = k_ref.shape[1]
    rows = mem_len * MEM_HEADS
    for g in range(q_ref.shape[0]):
        k2 = k_ref[g].reshape(rows, MEM_DH).astype(BF16)
        v2 = v_ref[g].reshape(rows, MEM_DH).astype(BF16)
        q_all = q_ref[g]
        qs = jnp.concatenate([q_all[:, h * MEM_DH:(h + 1) * MEM_DH] for h in range(MEM_HEADS)], axis=0)
        s = _dot_nt(qs, k2) * (MEM_DH ** -0.5)
        q_head = lax.div(lax.broadcasted_iota(jnp.int32, s.shape, 0), Q)
        k_head = lax.rem(lax.broadcasted_iota(jnp.int32, s.shape, 1), MEM_HEADS)
        s = jnp.where(q_head == k_head, s, MASKED)
        e = jnp.exp(s - jnp.max(s, axis=1, keepdims=True))
        p = e / jnp.sum(e, axis=1, keepdims=True)
        o = _dot(p.astype(BF16), v2)
        for h in range(MEM_HEADS):
            o_ref[g, :, h * MEM_DH:(h + 1) * MEM_DH] = o[h * Q:(h + 1) * Q, :].astype(BF16)


def _xattn_sample(q8, mem_k, mem_v, *, group):
    batch, qrows, d = q8.shape
    mem_len = mem_k.shape[1]
    mem = lambda: pl.BlockSpec((group, mem_len, MEM_HEADS, MEM_DH), lambda b: (b, 0, 0, 0))
    return pl.pallas_call(
        _xattn_sample_body,
        grid=(batch // group,),
        in_specs=[pl.BlockSpec((group, qrows, d), lambda b: (b, 0, 0)), mem(), mem()],
        out_specs=pl.BlockSpec((group, qrows, d), lambda b: (b, 0, 0)),
        out_shape=jax.ShapeDtypeStruct((batch, qrows, d), BF16),
        compiler_params=_cparams(("arbitrary",), 32),
        name="xattn_sample",
    )(q8, mem_k, mem_v)


def _pad_rows(a, rows):
    return jnp.pad(a, ((0, 0), (0, rows - a.shape[1]), (0, 0)))


def kernel(x_prompt, x_sample, mem_prompt, cache_attn_k, cache_attn_v, cache_mem_k, cache_mem_v, state_conv, state_C, state_n, state_m, g_mix, w_in, conv_w, conv_b, b_gates, g_mh, w_out, g_mem, w_mk, w_mv, g_xattn, w_mq, w_mo, g_ffn, w_gate, w_up, w_down, g_final):
    B, T, D = x_prompt.shape
    SB, ST, _ = x_sample.shape
    mem_len = mem_prompt.shape[1]
    W2 = 2 * MLSTM_W
    gate0 = COL_QA
    n_gate = 2 * MLSTM_HEADS

    w_main = jnp.concatenate([w_in[:, :gate0], w_in[:, gate0 + n_gate:]], axis=1).astype(BF16)
    w_gcols = jnp.pad(w_in[:, gate0:gate0 + n_gate], ((0, 0), (0, LANES - n_gate))).astype(BF16)
    bg_row = jnp.pad(b_gates, (0, LANES - n_gate)).reshape(1, LANES)
    w_mem = jnp.concatenate([w_mk, w_mv], axis=1).astype(BF16)
    w_out_m = w_out[:MLSTM_W].astype(BF16)
    w_out_a = w_out[MLSTM_W:].astype(BF16)
    w_mq_b, w_mo_b = w_mq.astype(BF16), w_mo.astype(BF16)
    w_gate_b, w_up_b, w_down_b = w_gate.astype(BF16), w_up.astype(BF16), w_down.astype(BF16)
    conv_b2 = conv_b.reshape(1, W2)
    g_mh2 = g_mh.reshape(1, MLSTM_W)

    xp = x_prompt.reshape(B * T, D)
    zmem = _norm_matmul(mem_prompt.reshape(B * mem_len, D), g_mem, w_mem, tm=B * mem_len, tn=512)
    z, gates = _norm_matmul(xp, g_mix, w_main, w_gcols, tm=2048, tn=512)
    hm, p_conv, p_C, p_n, p_m = _mlstm_prompt(z, gates, conv_w, conv_b2, bg_row, g_mh2,
                                              batch=B, seq=T, chunk=128, nchunk=4)
    ha = _dilated_prompt(z, batch=B, seq=T, tq=2048)
    x1, q = _outproj_q(xp, hm, ha, w_out_m, w_out_a, g_xattn, w_mq_b, tm=512)
    o = _xattn_prompt(q, zmem, batch=B, seq=T, mem_len=mem_len, tm=512)
    y_prompt = _post_ffn(x1, o, w_mo_b, g_ffn, w_gate_b, w_up_b, w_down_b, g_final, tm=512)

    keep = min(DILATED[-1][0], T)
    z3p = z.reshape(B, T, N_MAIN)
    p_attn_k = z3p[:, T - keep:, COL_KA:COL_KA + ATTN_W].reshape(B, keep, ATTN_HEADS, ATTN_DH)
    p_attn_v = z3p[:, T - keep:, COL_VA:COL_VA + ATTN_W].reshape(B, keep, ATTN_HEADS, ATTN_DH)
    p_mem_k = zmem[:, :D].reshape(B, mem_len, MEM_HEADS, MEM_DH)
    p_mem_v = zmem[:, D:].reshape(B, mem_len, MEM_HEADS, MEM_DH)

    xs = x_sample.reshape(SB * ST, D)
    zs, gates_s = _norm_matmul(xs, g_mix, w_main, w_gcols, tm=SB * ST, tn=512)
    zs3 = zs.reshape(SB, ST, N_MAIN)
    m0 = jnp.pad(state_m, ((0, 0), (0, LANES - MLSTM_HEADS))).reshape(SB, 1, LANES)
    hm_s, s_conv, s_C, s_n, s_m = _mlstm_sample(
        zs3, gates_s.reshape(SB, ST, LANES), state_conv, state_C, state_n, m0,
        conv_w, conv_b2, bg_row, g_mh2, group=8)
    qrows = 8

    def heads_first(col0):
        a = zs3[:, :, col0:col0 + ATTN_W].reshape(SB, ST, ATTN_HEADS, ATTN_DH).transpose(0, 2, 1, 3)
        return jnp.pad(a, ((0, 0), (0, 0), (0, qrows - ST), (0, 0)))

    ha_s = _dilated_sample(heads_first(COL_QA), heads_first(COL_KA), heads_first(COL_VA),
                           jnp.transpose(cache_attn_k, (0, 2, 3, 1)),
                           jnp.transpose(cache_attn_v, (0, 2, 3, 1)), steps=ST)
    ha_s = ha_s[:, :, :ST].transpose(0, 2, 1, 3)
    x1s, qs = _outproj_q(xs, hm_s.reshape(SB * ST, MLSTM_W), ha_s.reshape(SB * ST, ATTN_W),
                         w_out_m, w_out_a, g_xattn, w_mq_b, tm=SB * ST)
    os_ = _xattn_sample(_pad_rows(qs.reshape(SB, ST, D), qrows), cache_mem_k, cache_mem_v, group=2)[:, :ST]
    y_sample = _post_ffn(x1s, os_.reshape(SB * ST, D), w_mo_b, g_ffn, w_gate_b, w_up_b, w_down_b,
                         g_final, tm=512)

    s_attn_k = zs3[:, :, COL_KA:COL_KA + ATTN_W].reshape(SB, ST, ATTN_HEADS, ATTN_DH)
    s_attn_v = zs3[:, :, COL_VA:COL_VA + ATTN_W].reshape(SB, ST, ATTN_HEADS, ATTN_DH)

    return (y_prompt.reshape(B, T, D), y_sample.reshape(SB, ST, D), p_attn_k, p_attn_v,
            p_conv, p_C, p_n, p_m[:, :, 0], p_mem_k, p_mem_v,
            s_attn_k, s_attn_v, s_conv, s_C, s_n, s_m[:, :, 0])
```

```python
import functools

import jax
import jax.numpy as jnp
from jax import lax
from jax.experimental import pallas as pl
from jax.experimental.pallas import tpu as pltpu

F32 = jnp.float32
BF16 = jnp.bfloat16

EPS = 1e-6
MASKED = -1e30

MLSTM_HEADS = 4
MLSTM_DH = 128
MLSTM_W = MLSTM_HEADS * MLSTM_DH
ATTN_HEADS = 8
ATTN_DH = 64
ATTN_W = ATTN_HEADS * ATTN_DH
DILATED = ((128, 1), (512, 4), (2048, 16))
REL = 128
CONV_W = 4
MEM_HEADS = 4
MEM_DH = 256
LANES = 128
MIB = 1024 * 1024
PAD_PITCH = 24

COL_QK = 0
COL_VM = 2 * MLSTM_W
COL_OM = 3 * MLSTM_W
COL_QA = 4 * MLSTM_W
COL_KA = COL_QA + ATTN_W
COL_VA = COL_KA + ATTN_W
N_MAIN = COL_VA + ATTN_W


def _cparams(semantics, vmem_mib):
    return pltpu.CompilerParams(dimension_semantics=semantics, vmem_limit_bytes=vmem_mib * MIB)


def _rms(x, g):
    return x * lax.rsqrt(jnp.mean(x * x, axis=-1, keepdims=True) + EPS) * g


def _log_sigmoid(x):
    return jnp.minimum(x, 0.0) - jnp.log1p(jnp.exp(-jnp.abs(x)))


def _dot(a, b):
    return jnp.dot(a, b, preferred_element_type=F32)


def _dot_nt(a, b):
    return lax.dot_general(a, b, (((1,), (1,)), ((), ())), preferred_element_type=F32)


def _dot_tn(a, b):
    return lax.dot_general(a, b, (((0,), (0,)), ((), ())), preferred_element_type=F32)


def _norm_matmul_body(x_ref, g_ref, w_ref, *rest, with_side):
    if with_side:
        ws_ref, z_ref, side_ref, h_ref = rest
    else:
        z_ref, h_ref = rest

    @pl.when(pl.program_id(1) == 0)
    def _():
        h = _rms(x_ref[...], g_ref[...]).astype(BF16)
        h_ref[...] = h
        if with_side:
            side_ref[...] = _dot(h, ws_ref[...])

    z_ref[...] = _dot(h_ref[...], w_ref[...])


def _norm_matmul(x, g, w, w_side=None, *, tm, tn):
    n, d = x.shape
    c = w.shape[1]
    with_side = w_side is not None
    in_specs = [
        pl.BlockSpec((tm, d), lambda i, j: (i, 0)),
        pl.BlockSpec((1, d), lambda i, j: (0, 0)),
        pl.BlockSpec((d, tn), lambda i, j: (0, j)),
    ]
    out_specs = [pl.BlockSpec((tm, tn), lambda i, j: (i, j))]
    out_shape = [jax.ShapeDtypeStruct((n, c), F32)]
    args = [x, g.reshape(1, d), w]
    if with_side:
        in_specs.append(pl.BlockSpec((d, LANES), lambda i, j: (0, 0)))
        out_specs.append(pl.BlockSpec((tm, LANES), lambda i, j: (i, 0)))
        out_shape.append(jax.ShapeDtypeStruct((n, LANES), F32))
        args.append(w_side)
    vmem = 2 * tm * d * 4 + tm * d * 2 + 2 * d * tn * 2 + 2 * tm * tn * 4 + 4 * tm * LANES * 4
    res = pl.pallas_call(
        functools.partial(_norm_matmul_body, with_side=with_side),
        grid=(n // tm, c // tn),
        in_specs=in_specs,
        out_specs=out_specs,
        out_shape=out_shape,
        scratch_shapes=[pltpu.VMEM((tm, d), BF16)],
        compiler_params=_cparams(("arbitrary", "arbitrary"), vmem // MIB + 8),
        name="norm_matmul",
    )(*args)
    return res if with_side else res[0]


def _mlstm_prompt_body(qk_ref, v_ref, om_ref, gt_ref, cw_ref, cb_ref, bg_ref, gmh_ref,
                       hm_ref, conv_ref, c_out, n_out, m_out,
                       ubuf, c_s, n_s, m_s, *, chunk, nchunk):
    L = chunk
    H = MLSTM_HEADS
    rows_step = L * nchunk
    c = pl.program_id(1)
    last = pl.num_programs(1) - 1
    W2 = 2 * MLSTM_W

    @pl.when(c == 0)
    def _():
        ubuf[0:8, :] = jnp.zeros((8, W2), F32)
        c_s[...] = jnp.zeros_like(c_s)
        n_s[...] = jnp.zeros_like(n_s)
        m_s[...] = jnp.zeros_like(m_s)

    @pl.when(c > 0)
    def _():
        ubuf[0:8, :] = ubuf[rows_step:rows_step + 8, :]

    ubuf[8:rows_step + 8, :] = qk_ref[...]

    t_idx = lax.broadcasted_iota(jnp.int32, (L, L), 0)
    s_idx = lax.broadcasted_iota(jnp.int32, (L, L), 1)
    causal = s_idx <= t_idx
    tril = jnp.where(causal, 1.0, 0.0).astype(BF16)
    col = lax.broadcasted_iota(jnp.int32, (L, LANES), 1)

    units = [(ci, h) for ci in range(nchunk) for h in range(H)]
    pre = []
    for ci in range(nchunk):
        r0 = ci * L
        y = jnp.broadcast_to(cb_ref[...], (L, W2))
        for j in range(CONV_W):
            lo = r0 + 8 - (CONV_W - 1) + j
            y = y + ubuf[lo:lo + L, :] * cw_ref[j:j + 1, :]
        qk = y * jax.nn.sigmoid(y)
        gates = gt_ref[r0:r0 + L, :] + bg_ref[...]
        gates = jnp.where(col >= H, _log_sigmoid(gates), gates)
        g_hi = gates.astype(BF16)
        g_lo = (gates - g_hi.astype(F32)).astype(BF16)
        csum = _dot(tril, g_hi) + _dot(tril, g_lo)
        pre.append((qk, gates, csum, gates.T, csum.T))

    def unit(ci, h):
        qk, gates, csum, gates_t, csum_t = pre[ci]
        sl = slice(h * MLSTM_DH, (h + 1) * MLSTM_DH)
        qh = qk[:, sl]
        kh = qk[:, MLSTM_W + h * MLSTM_DH:MLSTM_W + (h + 1) * MLSTM_DH] * (MLSTM_DH ** -0.5)
        b_row = csum_t[H + h:H + h + 1, :]
        return dict(qh=qh, kh=kh, qb=qh.astype(BF16), kb=kh.astype(BF16),
                    vb=v_ref[ci * L:(ci + 1) * L, sl].astype(BF16),
                    i_col=gates[:, h:h + 1], i_row=gates_t[h:h + 1, :],
                    b_col=csum[:, H + h:H + h + 1], b_row=b_row, bl=b_row[:, L - 1:L])

    U = {u: unit(*u) for u in units}
    for d in U.values():
        d['dmat'] = jnp.where(causal, d['b_col'] - d['b_row'] + d['i_row'], MASKED)
        d['qk'] = _dot_nt(d['qb'], d['kb'])
        d['g_col'] = d['bl'] - d['b_col'] + d['i_col']
    for d in U.values():
        d['m_loc'] = jnp.max(d['dmat'], axis=1, keepdims=True)
        d['g_max'] = jnp.max(d['g_col'], axis=0, keepdims=True)
    for d in U.values():
        d['sc'] = d['qk'] * jnp.exp(d['dmat'] - d['m_loc'])
        d['kw'] = jnp.exp(d['g_col'] - d['g_max']) * d['kh']
    for d in U.values():
        d['a_loc'] = _dot(d['sc'].astype(BF16), d['vb'])
        d['r_loc'] = jnp.sum(d['sc'], axis=1, keepdims=True)
        d['dc_loc'] = _dot_tn(d['kw'].astype(BF16), d['vb'])
        d['dn_loc'] = jnp.sum(d['kw'], axis=0, keepdims=True)

    c_st = [c_s[h] for h in range(H)]
    n_st = [n_s[h:h + 1, :] for h in range(H)]
    m_st = [m_s[h:h + 1, 0:1] for h in range(H)]
    for (ci, h), d in U.items():
        m_prev, c_prev, n_prev = m_st[h], c_st[h], n_st[h]
        a_col = d['b_col'] + m_prev
        mt = jnp.maximum(a_col, d['m_loc'])
        wa = jnp.exp(a_col - mt)
        wl = jnp.exp(d['m_loc'] - mt)
        num = wa * _dot(d['qb'], c_prev.astype(BF16)) + wl * d['a_loc']
        den = wa * jnp.sum(d['qh'] * n_prev, axis=1, keepdims=True) + wl * d['r_loc']
        d['hh'] = num / jnp.maximum(jnp.abs(den), jnp.exp(-mt))
        m_new = jnp.maximum(d['bl'] + m_prev, d['g_max'])
        wc = jnp.exp(d['bl'] + m_prev - m_new)
        wg = jnp.exp(d['g_max'] - m_new)
        c_st[h] = wc * c_prev + wg * d['dc_loc']
        n_st[h] = wc * n_prev + wg * d['dn_loc']
        m_st[h] = m_new
    for h in range(H):
        c_s[h] = c_st[h]
        n_s[h:h + 1, :] = n_st[h]
        m_s[h:h + 1, :] = jnp.broadcast_to(m_st[h], (1, LANES))

    for (ci, h), d in U.items():
        sl = slice(h * MLSTM_DH, (h + 1) * MLSTM_DH)
        hh = d['hh']
        mu = jnp.mean(hh, axis=1, keepdims=True)
        var = jnp.mean(jnp.square(hh - mu), axis=1, keepdims=True)
        hn = (hh - mu) * lax.rsqrt(var + EPS) * gmh_ref[:, sl]
        hm_ref[ci * L:(ci + 1) * L, sl] = hn * jax.nn.sigmoid(om_ref[ci * L:(ci + 1) * L, sl])

    @pl.when(c == last)
    def _():
        conv_ref[0] = ubuf[rows_step + 8 - (CONV_W - 1):rows_step + 8, :]
        c_out[0] = c_s[...]
        n_out[0] = n_s[...]
        m_out[0] = m_s[...]


def _mlstm_prompt(z, gates, conv_w, conv_b, bg_row, g_mh, *, batch, seq, chunk, nchunk):
    rows = chunk * nchunk
    nc = seq // rows
    n = batch * seq
    row = lambda b, c: b * nc + c
    W2 = 2 * MLSTM_W
    full = lambda shape: pl.BlockSpec(shape, lambda b, c: (0,) * len(shape))
    return pl.pallas_call(
        functools.partial(_mlstm_prompt_body, chunk=chunk, nchunk=nchunk),
        grid=(batch, nc),
        in_specs=[
            pl.BlockSpec((rows, W2), lambda b, c: (row(b, c), COL_QK // W2)),
            pl.BlockSpec((rows, MLSTM_W), lambda b, c: (row(b, c), COL_VM // MLSTM_W)),
            pl.BlockSpec((rows, MLSTM_W), lambda b, c: (row(b, c), COL_OM // MLSTM_W)),
            pl.BlockSpec((rows, LANES), lambda b, c: (row(b, c), 0)),
            full((CONV_W, W2)), full((1, W2)), full((1, LANES)), full((1, MLSTM_W)),
        ],
        out_specs=[
            pl.BlockSpec((rows, MLSTM_W), lambda b, c: (row(b, c), 0)),
            pl.BlockSpec((1, CONV_W - 1, W2), lambda b, c: (b, 0, 0)),
            pl.BlockSpec((1, MLSTM_HEADS, MLSTM_DH, MLSTM_DH), lambda b, c: (b, 0, 0, 0)),
            pl.BlockSpec((1, MLSTM_HEADS, MLSTM_DH), lambda b, c: (b, 0, 0)),
            pl.BlockSpec((1, MLSTM_HEADS, LANES), lambda b, c: (b, 0, 0)),
        ],
        out_shape=[
            jax.ShapeDtypeStruct((n, MLSTM_W), F32),
            jax.ShapeDtypeStruct((batch, CONV_W - 1, W2), F32),
            jax.ShapeDtypeStruct((batch, MLSTM_HEADS, MLSTM_DH, MLSTM_DH), F32),
            jax.ShapeDtypeStruct((batch, MLSTM_HEADS, MLSTM_DH), F32),
            jax.ShapeDtypeStruct((batch, MLSTM_HEADS, LANES), F32),
        ],
        scratch_shapes=[
            pltpu.VMEM((rows + 8, W2), F32),
            pltpu.VMEM((MLSTM_HEADS, MLSTM_DH, MLSTM_DH), F32),
            pltpu.VMEM((MLSTM_HEADS, MLSTM_DH), F32),
            pltpu.VMEM((MLSTM_HEADS, LANES), F32),
        ],
        compiler_params=_cparams(("arbitrary", "arbitrary"), 32),
        name="mlstm_prompt",
    )(z, z, z, gates, conv_w, conv_b, bg_row, g_mh)


def _dilated_prompt_body(q_ref, k_ref, kp_ref, v_ref, vp_ref, o_ref, kpad, vpad, acc, mm, ll, bias, *, tq):
    i = pl.program_id(2)
    hd = ATTN_DH
    dmax = max(d for _, d in DILATED)
    ngrp = tq // dmax
    half = ngrp * PAD_PITCH
    cur = lax.rem(i, 2) * half
    prev = lax.rem(i + 1, 2) * half

    @pl.when(i == 0)
    def _():
        kpad[pl.ds(pl.multiple_of(prev, 8), half), :] = jnp.zeros((half, LANES), F32)
        vpad[pl.ds(pl.multiple_of(prev, 8), half), :] = jnp.zeros((half, LANES), F32)

    def pad_copy(g, carry):
        src = pl.ds(pl.multiple_of(g * dmax, dmax), dmax)
        dst = pl.ds(pl.multiple_of(cur + g * PAD_PITCH, 8), dmax)
        kpad[dst, :] = k_ref[src, :]
        vpad[dst, :] = v_ref[src, :]
        return carry

    lax.fori_loop(0, ngrp, pad_copy, 0, unroll=8)

    lane = lax.broadcasted_iota(jnp.int32, (REL, LANES), 1)
    first_head = lane < hd
    a_idx = lax.broadcasted_iota(jnp.int32, (REL, 2 * REL), 0)
    j_idx = lax.broadcasted_iota(jnp.int32, (REL, 2 * REL), 1)
    band = (j_idx >= a_idx) & (j_idx <= a_idx + REL)
    b0 = jnp.where(band, 0.0, MASKED)
    b1 = jnp.where(band & (j_idx >= jnp.where(i == 0, REL, 0)), 0.0, MASKED)
    bias[0] = jnp.concatenate([b0, b0], axis=0)
    bias[1] = jnp.concatenate([b1, b1], axis=0)

    order = sorted(DILATED, key=lambda wd: -wd[1])
    for bi, (window, d) in enumerate(order):
        assert window // d == REL and tq % (REL * d) == 0
        nblk = tq // (REL * d)
        for r in range(d):
            for cb in range(nblk):
                rows = pl.ds(r + d * REL * cb, REL, stride=d)
                q2 = q_ref[rows, :] * (hd ** -0.5)
                if d == dmax:
                    kp = kpad[pl.ds(prev + r, REL, stride=PAD_PITCH), :]
                    kc = kpad[pl.ds(cur + r, REL, stride=PAD_PITCH), :]
                    vp = vpad[pl.ds(prev + r, REL, stride=PAD_PITCH), :]
                    vc = vpad[pl.ds(cur + r, REL, stride=PAD_PITCH), :]
                    k2 = jnp.concatenate([kp, kc], axis=0)
                    v2 = jnp.concatenate([vp, vc], axis=0)
                elif cb == 0:
                    tail = pl.ds(r + d * REL * (nblk - 1), REL, stride=d)
                    k2 = jnp.concatenate([kp_ref[tail, :], k_ref[rows, :]], axis=0)
                    v2 = jnp.concatenate([vp_ref[tail, :], v_ref[rows, :]], axis=0)
                else:
                    both = pl.ds(r + d * REL * (cb - 1), 2 * REL, stride=d)
                    k2 = k_ref[both, :]
                    v2 = v_ref[both, :]
                qs = jnp.concatenate([jnp.where(first_head, q2, 0.0), jnp.where(first_head, 0.0, q2)], axis=0)
                s = _dot_nt(qs.astype(BF16), k2.astype(BF16)) + bias[1 if cb == 0 else 0]
                mx = jnp.max(s, axis=1, keepdims=True)
                p = jnp.exp(s - mx)
                l = jnp.sum(p, axis=1, keepdims=True)
                o = _dot(p.astype(BF16), v2.astype(BF16))
                m_blk = jnp.where(first_head, mx[0:REL], mx[REL:2 * REL])
                l_blk = jnp.where(first_head, l[0:REL], l[REL:2 * REL])
                o_blk = jnp.where(first_head, o[0:REL], o[REL:2 * REL])
                if bi == 0:
                    mm[rows, :] = m_blk
                    ll[rows, :] = l_blk
                    acc[rows, :] = o_blk
                else:
                    m_old = mm[rows, :]
                    m_new = jnp.maximum(m_old, m_blk)
                    w_old = jnp.exp(m_old - m_new)
                    w_blk = jnp.exp(m_blk - m_new)
                    mm[rows, :] = m_new
                    ll[rows, :] = ll[rows, :] * w_old + l_blk * w_blk
                    acc[rows, :] = acc[rows, :] * w_old + o_blk * w_blk

    o_ref[...] = acc[...] / ll[...]


def _dilated_prompt(z, *, batch, seq, tq):
    nt = seq // tq
    n = batch * seq
    pairs = ATTN_W // LANES
    dmax = max(d for _, d in DILATED)
    assert tq == REL * dmax
    pad_rows = 2 * (tq // dmax) * PAD_PITCH

    def spec(col0, back):
        return pl.BlockSpec((tq, LANES),
                            lambda b, p, i: (b * nt + jnp.maximum(i - back, 0), col0 // LANES + p))

    return pl.pallas_call(
        functools.partial(_dilated_prompt_body, tq=tq),
        grid=(batch, pairs, nt),
        in_specs=[spec(COL_QA, 0), spec(COL_KA, 0), spec(COL_KA, 1), spec(COL_VA, 0), spec(COL_VA, 1)],
        out_specs=pl.BlockSpec((tq, LANES), lambda b, p, i: (b * nt + i, p)),
        out_shape=jax.ShapeDtypeStruct((n, ATTN_W), F32),
        scratch_shapes=[
            pltpu.VMEM((pad_rows, LANES), F32), pltpu.VMEM((pad_rows, LANES), F32),
            pltpu.VMEM((tq, LANES), F32), pltpu.VMEM((tq, LANES), F32), pltpu.VMEM((tq, LANES), F32),
            pltpu.VMEM((2, 2 * REL, 2 * REL), F32),
        ],
        compiler_params=_cparams(("arbitrary", "arbitrary", "arbitrary"), 32),
        name="dilated_prompt",
    )(z, z, z, z, z)


def _outproj_q_body(x_ref, hm_ref, ha_ref, wom_ref, woa_ref, gx_ref, wq_ref, x1_ref, q_ref):
    x1 = x_ref[...] + _dot(hm_ref[...].astype(BF16), wom_ref[...]) + _dot(ha_ref[...].astype(BF16), woa_ref[...])
    x1_ref[...] = x1
    q_ref[...] = _dot(_rms(x1, gx_ref[...]).astype(BF16), wq_ref[...]).astype(BF16)


def _outproj_q(x, hm, ha, w_out_m, w_out_a, g_x, w_mq, *, tm):
    n, d = x.shape
    row = lambda w: pl.BlockSpec((tm, w), lambda i: (i, 0))
    full = lambda shape: pl.BlockSpec(shape, lambda i: (0, 0))
    return pl.pallas_call(
        _outproj_q_body,
        grid=(n // tm,),
        in_specs=[row(d), row(MLSTM_W), row(ATTN_W), full((MLSTM_W, d)), full((ATTN_W, d)),
                  full((1, d)), full((d, d))],
        out_specs=[row(d), row(d)],
        out_shape=[jax.ShapeDtypeStruct((n, d), F32), jax.ShapeDtypeStruct((n, d), BF16)],
        compiler_params=_cparams(("arbitrary",), 40),
        name="outproj_q",
    )(x, hm, ha, w_out_m, w_out_a, g_x.reshape(1, d), w_mq)


def _xattn_prompt_body(q_ref, k_ref, v_ref, o_ref):
    for h in range(MEM_HEADS):
        sl = slice(h * MEM_DH, (h + 1) * MEM_DH)
        s = _dot_nt(q_ref[:, sl], k_ref[:, sl].astype(BF16)) * (MEM_DH ** -0.5)
        e = jnp.exp(s - jnp.max(s, axis=1, keepdims=True))
        p = e / jnp.sum(e, axis=1, keepdims=True)
        o_ref[:, sl] = _dot(p.astype(BF16), v_ref[:, sl].astype(BF16)).astype(BF16)


def _xattn_prompt(q, zmem, *, batch, seq, mem_len, tm):
    n, d = q.shape
    nt = seq // tm
    return pl.pallas_call(
        _xattn_prompt_body,
        grid=(batch, nt),
        in_specs=[
            pl.BlockSpec((tm, d), lambda b, i: (b * nt + i, 0)),
            pl.BlockSpec((mem_len, d), lambda b, i: (b, 0)),
            pl.BlockSpec((mem_len, d), lambda b, i: (b, 1)),
        ],
        out_specs=pl.BlockSpec((tm, d), lambda b, i: (b * nt + i, 0)),
        out_shape=jax.ShapeDtypeStruct((n, d), BF16),
        compiler_params=_cparams(("arbitrary", "arbitrary"), 32),
        name="xattn_prompt",
    )(q, zmem, zmem)


def _ffn_steps(x1_ref, o_ref, wmo_ref, gf_ref, wg_ref, wu_ref, wd_ref, gfin_ref, y_ref):
    x2 = x1_ref[...] + _dot(o_ref[...], wmo_ref[...])
    h = _rms(x2, gf_ref[...]).astype(BF16)
    yield
    g = _dot(h, wg_ref[...])
    yield
    u = _dot(h, wu_ref[...])
    yield
    a = (g * jax.nn.sigmoid(g) * u).astype(BF16)
    yield
    acc = x2 + _dot(a, wd_ref[...])
    yield
    y_ref[...] = _rms(acc, gfin_ref[...])


def _run(*gens):
    gens = list(gens)
    while gens:
        alive = []
        for g in gens:
            try:
                next(g)
                alive.append(g)
            except StopIteration:
                pass
        gens = alive


def _post_ffn_body(*refs):
    _run(_ffn_steps(*refs))


def _post_ffn(x1, o, w_mo, g_ffn, w_gate, w_up, w_down, g_final, *, tm):
    n, d = x1.shape
    dff = w_gate.shape[1]
    tm = min(tm, n)
    assert n % tm == 0
    row = lambda: pl.BlockSpec((tm, d), lambda i: (i, 0))
    full = lambda shape: pl.BlockSpec(shape, lambda i: (0, 0), pipeline_mode=pl.Buffered(1))
    weights = 2 * (d * d + 3 * d * dff)
    work = tm * (6 * d * 4 + 2 * dff * 4 + dff * 2 + d * 2)
    return pl.pallas_call(
        _post_ffn_body,
        grid=(n // tm,),
        in_specs=[row(), row(), full((d, d)), full((1, d)), full((d, dff)), full((d, dff)),
                  full((dff, d)), full((1, d))],
        out_specs=row(),
        out_shape=jax.ShapeDtypeStruct((n, d), F32),
        compiler_params=_cparams(("arbitrary",), (weights + work) // MIB + 8),
        name="post_ffn",
    )(x1, o, w_mo, g_ffn.reshape(1, d), w_gate, w_up, w_down, g_final.reshape(1, d))


def _mlstm_sample_body(qk_ref, v_ref, om_ref, gt_ref, conv_ref, c_ref, n_ref, m_ref,
                       cw_ref, cb_ref, bg_ref, gmh_ref,
                       hm_ref, conv_out, c_out, n_out, m_out, *, group, steps):
    T = steps
    W2 = 2 * MLSTM_W
    t_idx = lax.broadcasted_iota(jnp.int32, (T, T), 0)
    s_idx = lax.broadcasted_iota(jnp.int32, (T, T), 1)
    causal = s_idx <= t_idx
    eye = s_idx == t_idx
    col = lax.broadcasted_iota(jnp.int32, (T, LANES), 1)
    pad = jnp.zeros((8 - T, MLSTM_DH), F32)

    def to_row(v_col):
        return jnp.sum(jnp.where(eye, v_col, 0.0), axis=0, keepdims=True)

    def one(b, carry):
        up = jnp.concatenate([conv_ref[b], qk_ref[b]], axis=0)
        y = jnp.broadcast_to(cb_ref[...], (T, W2))
        for j in range(CONV_W):
            y = y + up[j:j + T, :] * cw_ref[j:j + 1, :]
        conv_out[b] = up[T:T + CONV_W - 1, :]
        qk = y * jax.nn.sigmoid(y)
        gates = gt_ref[b] + bg_ref[...]
        gates = jnp.where(col >= MLSTM_HEADS, _log_sigmoid(gates), gates)
        v_all = v_ref[b]
        om_all = om_ref[b]
        m_all = m_ref[b]
        hs = []
        for h in range(MLSTM_HEADS):
            sl = slice(h * MLSTM_DH, (h + 1) * MLSTM_DH)
            qh = qk[:, sl]
            kh = qk[:, MLSTM_W + h * MLSTM_DH:MLSTM_W + (h + 1) * MLSTM_DH] * (MLSTM_DH ** -0.5)
            vh = v_all[:, sl]
            i_col = gates[:, h:h + 1]
            f_col = gates[:, MLSTM_HEADS + h:MLSTM_HEADS + h + 1]
            i_row = to_row(i_col)
            f_row = to_row(f_col)
            m_prev = m_all[:, h:h + 1]
            c_prev = c_ref[b, h]
            n_prev = n_ref[b, h:h + 1, :]

            b_col = jnp.sum(jnp.where(causal, f_row, 0.0), axis=1, keepdims=True)
            b_row = to_row(b_col)
            dmat = jnp.where(causal, b_col - b_row + i_row, MASKED)
            a_col = b_col + m_prev
            mt = jnp.maximum(a_col, jnp.max(dmat, axis=1, keepdims=True))
            wa = jnp.exp(a_col - mt)
            decay = jnp.exp(dmat - mt)
            qkt = jnp.zeros((T, T), F32)
            for s in range(T):
                dots = jnp.sum(qh * kh[s:s + 1, :], axis=1, keepdims=True)
                qkt = jnp.where(s_idx == s, dots, qkt)
            sc = qkt * decay
            q8 = jnp.concatenate([qh, pad], axis=0).astype(BF16)
            inter = _dot(q8, c_prev.astype(BF16))[0:T, :]
            intra = jnp.zeros((T, MLSTM_DH), F32)
            for s in range(T):
                intra = intra + sc[:, s:s + 1] * vh[s:s + 1, :]
            num = wa * inter + intra
            den = wa * jnp.sum(qh * n_prev, axis=1, keepdims=True) + jnp.sum(sc, axis=1, keepdims=True)
            hh = num / jnp.maximum(jnp.abs(den), jnp.exp(-mt))

            bl = b_col[T - 1:T, :]
            g_col = bl - b_col + i_col
            m_new = jnp.maximum(bl + m_prev, jnp.max(g_col, axis=0, keepdims=True))
            wc = jnp.exp(bl + m_prev - m_new)
            kw = jnp.exp(g_col - m_new) * kh
            kw8 = jnp.concatenate([kw, pad], axis=0).astype(BF16)
            v8 = jnp.concatenate([vh, pad], axis=0).astype(BF16)
            c_out[b, h] = wc * c_prev + _dot_tn(kw8, v8)
            n_out[b, h:h + 1, :] = wc * n_prev + jnp.sum(kw, axis=0, keepdims=True)
            m_out[b, h:h + 1, :] = jnp.broadcast_to(m_new, (1, LANES))

            mu = jnp.mean(hh, axis=1, keepdims=True)
            var = jnp.mean(jnp.square(hh - mu), axis=1, keepdims=True)
            hn = (hh - mu) * lax.rsqrt(var + EPS) * gmh_ref[:, sl]
            hs.append(hn * jax.nn.sigmoid(om_all[:, sl]))
        hm_ref[b] = jnp.concatenate(hs, axis=1)
        return carry

    lax.fori_loop(0, group, one, 0, unroll=2)


def _mlstm_sample(z3, gates3, conv, c0, n0, m0, conv_w, conv_b, bg_row, g_mh, *, group):
    batch, steps, _ = z3.shape
    W2 = 2 * MLSTM_W
    full = lambda shape: pl.BlockSpec(shape, lambda g: (0,) * len(shape))
    return pl.pallas_call(
        functools.partial(_mlstm_sample_body, group=group, steps=steps),
        grid=(batch // group,),
        in_specs=[
            pl.BlockSpec((group, steps, W2), lambda g: (g, 0, COL_QK // W2)),
            pl.BlockSpec((group, steps, MLSTM_W), lambda g: (g, 0, COL_VM // MLSTM_W)),
            pl.BlockSpec((group, steps, MLSTM_W), lambda g: (g, 0, COL_OM // MLSTM_W)),
            pl.BlockSpec((group, steps, LANES), lambda g: (g, 0, 0)),
            pl.BlockSpec((group, CONV_W - 1, W2), lambda g: (g, 0, 0)),
            pl.BlockSpec((group, MLSTM_HEADS, MLSTM_DH, MLSTM_DH), lambda g: (g, 0, 0, 0)),
            pl.BlockSpec((group, MLSTM_HEADS, MLSTM_DH), lambda g: (g, 0, 0)),
            pl.BlockSpec((group, 1, LANES), lambda g: (g, 0, 0)),
            full((CONV_W, W2)), full((1, W2)), full((1, LANES)), full((1, MLSTM_W)),
        ],
        out_specs=[
            pl.BlockSpec((group, steps, MLSTM_W), lambda g: (g, 0, 0)),
            pl.BlockSpec((group, CONV_W - 1, W2), lambda g: (g, 0, 0)),
            pl.BlockSpec((group, MLSTM_HEADS, MLSTM_DH, MLSTM_DH), lambda g: (g, 0, 0, 0)),
            pl.BlockSpec((group, MLSTM_HEADS, MLSTM_DH), lambda g: (g, 0, 0)),
            pl.BlockSpec((group, MLSTM_HEADS, LANES), lambda g: (g, 0, 0)),
        ],
        out_shape=[
            jax.ShapeDtypeStruct((batch, steps, MLSTM_W), F32),
            jax.ShapeDtypeStruct((batch, CONV_W - 1, W2), F32),
            jax.ShapeDtypeStruct((batch, MLSTM_HEADS, MLSTM_DH, MLSTM_DH), F32),
            jax.ShapeDtypeStruct((batch, MLSTM_HEADS, MLSTM_DH), F32),
            jax.ShapeDtypeStruct((batch, MLSTM_HEADS, LANES), F32),
        ],
        compiler_params=_cparams(("arbitrary",), 32),
        name="mlstm_sample",
    )(z3, z3, z3, gates3, conv, c0, n0, m0, conv_w, conv_b, bg_row, g_mh)


def _branch_count(dist):
    cnt = jnp.zeros(dist.shape, F32)
    for window, d in DILATED:
        ok = (dist >= 0) & (dist <= window) & (lax.rem(jnp.maximum(dist, 0), d) == 0)
        cnt = cnt + jnp.where(ok, 1.0, 0.0)
    return cnt


def _dilated_sample_steps(q_ref, kn_ref, vn_ref, kc_ref, vc_ref, o_ref, ot_s, fresh_s, den_s, *, steps, wb):
    Q = q_ref.shape[2]
    t_c = lax.broadcasted_iota(jnp.int32, (Q, wb), 0)
    pos_c = lax.broadcasted_iota(jnp.int32, (Q, wb), 1)
    cnt_c = _branch_count(wb + t_c - pos_c)
    t_n = lax.broadcasted_iota(jnp.int32, (Q, Q), 0)
    s_n = lax.broadcasted_iota(jnp.int32, (Q, Q), 1)
    cnt_n = jnp.where(s_n < steps, _branch_count(t_n - s_n), 0.0)
    lane_t = lax.broadcasted_iota(jnp.int32, (ATTN_DH, LANES), 1)

    def head(h):
        qh = q_ref[0, h] * (ATTN_DH ** -0.5)
        knh = kn_ref[0, h]
        vnh = vn_ref[0, h]
        s_c = _dot(qh.astype(BF16), kc_ref[0, h].astype(BF16))
        s_new = jnp.zeros((Q, Q), F32)
        for s in range(steps):
            dots = jnp.sum(qh * knh[s:s + 1, :], axis=1, keepdims=True)
            s_new = jnp.where(s_n == s, dots, s_new)
        s_c = jnp.where(cnt_c > 0, s_c, MASKED)
        s_new = jnp.where(cnt_n > 0, s_new, MASKED)
        mx = jnp.maximum(jnp.max(s_c, axis=1, keepdims=True), jnp.max(s_new, axis=1, keepdims=True))
        p_c = cnt_c * jnp.exp(s_c - mx)
        p_n = cnt_n * jnp.exp(s_new - mx)
        den = jnp.sum(p_c, axis=1, keepdims=True) + jnp.sum(p_n, axis=1, keepdims=True)
        den_s[h] = jnp.broadcast_to(den, (Q, ATTN_DH))
        part = [jnp.zeros((ATTN_DH, LANES), F32) for _ in range(steps)]
        for c in range(wb // LANES):
            cs = slice(c * LANES, (c + 1) * LANES)
            v_c = vc_ref[0, h, :, cs]
            for t in range(steps):
                part[t] = part[t] + v_c * p_c[t:t + 1, cs]
        o_t = jnp.zeros((ATTN_DH, LANES), F32)
        for t in range(steps):
            o_t = jnp.where(lane_t == t, jnp.sum(part[t], axis=1, keepdims=True), o_t)
        ot_s[h * ATTN_DH:(h + 1) * ATTN_DH, :] = o_t
        o_new = jnp.zeros((Q, ATTN_DH), F32)
        for s in range(steps):
            o_new = o_new + p_n[:, s:s + 1] * vnh[s:s + 1, :]
        fresh_s[h] = o_new

    for h in range(ATTN_HEADS):
        head(h)
        yield
    o_cached = ot_s[...].T
    for h in range(ATTN_HEADS):
        o_ref[0, h] = (o_cached[0:Q, h * ATTN_DH:(h + 1) * ATTN_DH] + fresh_s[h]) / den_s[h]


def _ffn_and_sample_attn_body(x1_ref, o_ref, wmo_ref, gf_ref, wg_ref, wu_ref, wd_ref, gfin_ref,
                              q_ref, kn_ref, vn_ref, kc_ref, vc_ref,
                              y_ref, oa_ref, ot_s, fresh_s, den_s, *, steps, wb):
    _run(_ffn_steps(x1_ref, o_ref, wmo_ref, gf_ref, wg_ref, wu_ref, wd_ref, gfin_ref, y_ref),
         _dilated_sample_steps(q_ref, kn_ref, vn_ref, kc_ref, vc_ref, oa_ref, ot_s, fresh_s, den_s,
                               steps=steps, wb=wb))


def _ffn_and_sample_attn(x1, o, w_mo, g_ffn, w_gate, w_up, w_down, g_final, q8, kn8, vn8, kc_t, vc_t, *, steps):
    n, d = x1.shape
    dff = w_gate.shape[1]
    batch, _, qrows, _ = q8.shape
    wb = kc_t.shape[-1]
    assert n % batch == 0
    tm = n // batch
    row = lambda: pl.BlockSpec((tm, d), lambda i: (i, 0))
    full = lambda shape: pl.BlockSpec(shape, lambda i: (0, 0), pipeline_mode=pl.Buffered(1))
    new = lambda: pl.BlockSpec((1, ATTN_HEADS, qrows, ATTN_DH), lambda b: (b, 0, 0, 0))
    cache = lambda: pl.BlockSpec((1, ATTN_HEADS, ATTN_DH, wb), lambda b: (b, 0, 0, 0))
    weights = 2 * (d * d + 3 * d * dff)
    work = tm * (6 * d * 4 + 2 * dff * 4 + dff * 2 + d * 2)
    caches = 2 * 2 * ATTN_W * wb * 4
    return pl.pallas_call(
        functools.partial(_ffn_and_sample_attn_body, steps=steps, wb=wb),
        grid=(batch,),
        in_specs=[row(), row(), full((d, d)), full((1, d)), full((d, dff)), full((d, dff)),
                  full((dff, d)), full((1, d)), new(), new(), new(), cache(), cache()],
        out_specs=[row(), new()],
        out_shape=[jax.ShapeDtypeStruct((n, d), F32),
                   jax.ShapeDtypeStruct((batch, ATTN_HEADS, qrows, ATTN_DH), F32)],
        scratch_shapes=[pltpu.VMEM((ATTN_W, LANES), F32),
                        pltpu.VMEM((ATTN_HEADS, qrows, ATTN_DH), F32),
                        pltpu.VMEM((ATTN_HEADS, qrows, ATTN_DH), F32)],
        compiler_params=_cparams(("arbitrary",), (weights + work + caches) // MIB + 10),
        name="ffn_and_sample_attn",
    )(x1, o, w_mo, g_ffn.reshape(1, d), w_gate, w_up, w_down, g_final.reshape(1, d), q8, kn8, vn8, kc_t, vc_t)


def _xattn_sample_body(q_ref, k_ref, v_ref, o_ref):
    Q = q_ref.shape[1]
    mem_len = k_ref.shape[1]
    rows = mem_len * MEM_HEADS
    for g in range(q_ref.shape[0]):
        k2 = k_ref[g].reshape(rows, MEM_DH).astype(BF16)
        v2 = v_ref[g].reshape(rows, MEM_DH).astype(BF16)
        q_all = q_ref[g]
        qs = jnp.concatenate([q_all[:, h * MEM_DH:(h + 1) * MEM_DH] for h in range(MEM_HEADS)], axis=0)
        s = _dot_nt(qs, k2) * (MEM_DH ** -0.5)
        q_head = lax.div(lax.broadcasted_iota(jnp.int32, s.shape, 0), Q)
        k_head = lax.rem(lax.broadcasted_iota(jnp.int32, s.shape, 1), MEM_HEADS)
        s = jnp.where(q_head == k_head, s, MASKED)
        e = jnp.exp(s - jnp.max(s, axis=1, keepdims=True))
        p = e / jnp.sum(e, axis=1, keepdims=True)
        o = _dot(p.astype(BF16), v2)
        for h in range(MEM_HEADS):
            o_ref[g, :, h * MEM_DH:(h + 1) * MEM_DH] = o[h * Q:(h + 1) * Q, :].astype(BF16)


def _xattn_sample(q8, mem_k, mem_v, *, group):
    batch, qrows, d = q8.shape
    mem_len = mem_k.shape[1]
    mem = lambda: pl.BlockSpec((group, mem_len, MEM_HEADS, MEM_DH), lambda b: (b, 0, 0, 0))
    return pl.pallas_call(
        _xattn_sample_body,
        grid=(batch // group,),
        in_specs=[pl.BlockSpec((group, qrows, d), lambda b: (b, 0, 0)), mem(), mem()],
        out_specs=pl.BlockSpec((group, qrows, d), lambda b: (b, 0, 0)),
        out_shape=jax.ShapeDtypeStruct((batch, qrows, d), BF16),
        compiler_params=_cparams(("arbitrary",), 32),
        name="xattn_sample",
    )(q8, mem_k, mem_v)


def _pad_rows(a, rows):
    return jnp.pad(a, ((0, 0), (0, rows - a.shape[1]), (0, 0)))


def kernel(x_prompt, x_sample, mem_prompt, cache_attn_k, cache_attn_v, cache_mem_k, cache_mem_v, state_conv, state_C, state_n, state_m, g_mix, w_in, conv_w, conv_b, b_gates, g_mh, w_out, g_mem, w_mk, w_mv, g_xattn, w_mq, w_mo, g_ffn, w_gate, w_up, w_down, g_final):
    B, T, D = x_prompt.shape
    SB, ST, _ = x_sample.shape
    mem_len = mem_prompt.shape[1]
    W2 = 2 * MLSTM_W
    gate0 = COL_QA
    n_gate = 2 * MLSTM_HEADS

    w_main = jnp.concatenate([w_in[:, :gate0], w_in[:, gate0 + n_gate:]], axis=1).astype(BF16)
    w_gcols = jnp.pad(w_in[:, gate0:gate0 + n_gate], ((0, 0), (0, LANES - n_gate))).astype(BF16)
    bg_row = jnp.pad(b_gates, (0, LANES - n_gate)).reshape(1, LANES)
    w_mem = jnp.concatenate([w_mk, w_mv], axis=1).astype(BF16)
    w_out_m = w_out[:MLSTM_W].astype(BF16)
    w_out_a = w_out[MLSTM_W:].astype(BF16)
    w_mq_b, w_mo_b = w_mq.astype(BF16), w_mo.astype(BF16)
    w_gate_b, w_up_b, w_down_b = w_gate.astype(BF16), w_up.astype(BF16), w_down.astype(BF16)
    conv_b2 = conv_b.reshape(1, W2)
    g_mh2 = g_mh.reshape(1, MLSTM_W)

    xp = x_prompt.reshape(B * T, D)
    zmem = _norm_matmul(mem_prompt.reshape(B * mem_len, D), g_mem, w_mem, tm=B * mem_len, tn=512)
    z, gates = _norm_matmul(xp, g_mix, w_main, w_gcols, tm=2048, tn=512)
    hm, p_conv, p_C, p_n, p_m = _mlstm_prompt(z, gates, conv_w, conv_b2, bg_row, g_mh2,
                                              batch=B, seq=T, chunk=128, nchunk=4)
    ha = _dilated_prompt(z, batch=B, seq=T, tq=2048)
    x1, q = _outproj_q(xp, hm, ha, w_out_m, w_out_a, g_xattn, w_mq_b, tm=512)
    o = _xattn_prompt(q, zmem, batch=B, seq=T, mem_len=mem_len, tm=512)

    keep = min(DILATED[-1][0], T)
    z3p = z.reshape(B, T, N_MAIN)
    p_attn_k = z3p[:, T - keep:, COL_KA:COL_KA + ATTN_W].reshape(B, keep, ATTN_HEADS, ATTN_DH)
    p_attn_v = z3p[:, T - keep:, COL_VA:COL_VA + ATTN_W].reshape(B, keep, ATTN_HEADS, ATTN_DH)
    p_mem_k = zmem[:, :D].reshape(B, mem_len, MEM_HEADS, MEM_DH)
    p_mem_v = zmem[:, D:].reshape(B, mem_len, MEM_HEADS, MEM_DH)

    xs = x_sample.reshape(SB * ST, D)
    zs, gates_s = _norm_matmul(xs, g_mix, w_main, w_gcols, tm=SB * ST, tn=512)
    zs3 = zs.reshape(SB, ST, N_MAIN)
    m0 = jnp.pad(state_m, ((0, 0), (0, LANES - MLSTM_HEADS))).reshape(SB, 1, LANES)
    hm_s, s_conv, s_C, s_n, s_m = _mlstm_sample(
        zs3, gates_s.reshape(SB, ST, LANES), state_conv, state_C, state_n, m0,
        conv_w, conv_b2, bg_row, g_mh2, group=8)
    qrows = 8

    def heads_first(col0):
        a = zs3[:, :, col0:col0 + ATTN_W].reshape(SB, ST, ATTN_HEADS, ATTN_DH).transpose(0, 2, 1, 3)
        return jnp.pad(a, ((0, 0), (0, 0), (0, qrows - ST), (0, 0)))

    y_prompt, ha_s = _ffn_and_sample_attn(
        x1, o, w_mo_b, g_ffn, w_gate_b, w_up_b, w_down_b, g_final,
        heads_first(COL_QA), heads_first(COL_KA), heads_first(COL_VA),
        jnp.transpose(cache_attn_k, (0, 2, 3, 1)), jnp.transpose(cache_attn_v, (0, 2, 3, 1)), steps=ST)
    ha_s = ha_s[:, :, :ST].transpose(0, 2, 1, 3)
    x1s, qs = _outproj_q(xs, hm_s.reshape(SB * ST, MLSTM_W), ha_s.reshape(SB * ST, ATTN_W),
                         w_out_m, w_out_a, g_xattn, w_mq_b, tm=SB * ST)
    os_ = _xattn_sample(_pad_rows(qs.reshape(SB, ST, D), qrows), cache_mem_k, cache_mem_v, group=4)[:, :ST]
    y_sample = _post_ffn(x1s, os_.reshape(SB * ST, D), w_mo_b, g_ffn, w_gate_b, w_up_b, w_down_b,
                         g_final, tm=512)

    s_attn_k = zs3[:, :, COL_KA:COL_KA + ATTN_W].reshape(SB, ST, ATTN_HEADS, ATTN_DH)
    s_attn_v = zs3[:, :, COL_VA:COL_VA + ATTN_W].reshape(SB, ST, ATTN_HEADS, ATTN_DH)

    return (y_prompt.reshape(B, T, D), y_sample.reshape(SB, ST, D), p_attn_k, p_attn_v,
            p_conv, p_C, p_n, p_m[:, :, 0], p_mem_k, p_mem_v,
            s_attn_k, s_attn_v, s_conv, s_C, s_n, s_m[:, :, 0])
```

```python
import functools

import jax
import jax.numpy as jnp
from jax import lax
from jax.experimental import pallas as pl
from jax.experimental.pallas import tpu as pltpu

F32 = jnp.float32
BF16 = jnp.bfloat16

EPS = 1e-6
MASKED = -1e30

MLSTM_HEADS = 4
MLSTM_DH = 128
MLSTM_W = MLSTM_HEADS * MLSTM_DH
ATTN_HEADS = 8
ATTN_DH = 64
ATTN_W = ATTN_HEADS * ATTN_DH
DILATED = ((128, 1), (512, 4), (2048, 16))
REL = 128
CONV_W = 4
MEM_HEADS = 4
MEM_DH = 256
LANES = 128
MIB = 1024 * 1024
PAD_PITCH = 24

COL_QK = 0
COL_VM = 2 * MLSTM_W
COL_OM = 3 * MLSTM_W
COL_QA = 4 * MLSTM_W
COL_KA = COL_QA + ATTN_W
COL_VA = COL_KA + ATTN_W
N_MAIN = COL_VA + ATTN_W


def _cparams(semantics, vmem_mib):
    return pltpu.CompilerParams(dimension_semantics=semantics, vmem_limit_bytes=vmem_mib * MIB)


def _rms(x, g):
    return x * lax.rsqrt(jnp.mean(x * x, axis=-1, keepdims=True) + EPS) * g


def _log_sigmoid(x):
    return jnp.minimum(x, 0.0) - jnp.log1p(jnp.exp(-jnp.abs(x)))


def _dot(a, b):
    return jnp.dot(a, b, preferred_element_type=F32)


def _dot_nt(a, b):
    return lax.dot_general(a, b, (((1,), (1,)), ((), ())), preferred_element_type=F32)


def _dot_tn(a, b):
    return lax.dot_general(a, b, (((0,), (0,)), ((), ())), preferred_element_type=F32)


def _norm_matmul_body(x_ref, g_ref, w_ref, *rest, with_side):
    if with_side:
        ws_ref, z_ref, side_ref, h_ref = rest
    else:
        z_ref, h_ref = rest

    @pl.when(pl.program_id(1) == 0)
    def _():
        h = _rms(x_ref[...], g_ref[...]).astype(BF16)
        h_ref[...] = h
        if with_side:
            side_ref[...] = _dot(h, ws_ref[...])

    z_ref[...] = _dot(h_ref[...], w_ref[...])


def _norm_matmul(x, g, w, w_side=None, *, tm, tn):
    n, d = x.shape
    c = w.shape[1]
    with_side = w_side is not None
    in_specs = [
        pl.BlockSpec((tm, d), lambda i, j: (i, 0)),
        pl.BlockSpec((1, d), lambda i, j: (0, 0)),
        pl.BlockSpec((d, tn), lambda i, j: (0, j)),
    ]
    out_specs = [pl.BlockSpec((tm, tn), lambda i, j: (i, j))]
    out_shape = [jax.ShapeDtypeStruct((n, c), F32)]
    args = [x, g.reshape(1, d), w]
    if with_side:
        in_specs.append(pl.BlockSpec((d, LANES), lambda i, j: (0, 0)))
        out_specs.append(pl.BlockSpec((tm, LANES), lambda i, j: (i, 0)))
        out_shape.append(jax.ShapeDtypeStruct((n, LANES), F32))
        args.append(w_side)
    vmem = 2 * tm * d * 4 + tm * d * 2 + 2 * d * tn * 2 + 2 * tm * tn * 4 + 4 * tm * LANES * 4
    res = pl.pallas_call(
        functools.partial(_norm_matmul_body, with_side=with_side),
        grid=(n // tm, c // tn),
        in_specs=in_specs,
        out_specs=out_specs,
        out_shape=out_shape,
        scratch_shapes=[pltpu.VMEM((tm, d), BF16)],
        compiler_params=_cparams(("arbitrary", "arbitrary"), vmem // MIB + 8),
        name="norm_matmul",
    )(*args)
    return res if with_side else res[0]


def _mlstm_prompt_body(qk_ref, v_ref, om_ref, gt_ref, cw_ref, cb_ref, bg_ref, gmh_ref,
                       hm_ref, conv_ref, c_out, n_out, m_out,
                       ubuf, c_s, n_s, m_s, *, chunk, nchunk):
    L = chunk
    H = MLSTM_HEADS
    rows_step = L * nchunk
    c = pl.program_id(1)
    last = pl.num_programs(1) - 1
    W2 = 2 * MLSTM_W

    @pl.when(c == 0)
    def _():
        ubuf[0:8, :] = jnp.zeros((8, W2), F32)
        c_s[...] = jnp.zeros_like(c_s)
        n_s[...] = jnp.zeros_like(n_s)
        m_s[...] = jnp.zeros_like(m_s)

    @pl.when(c > 0)
    def _():
        ubuf[0:8, :] = ubuf[rows_step:rows_step + 8, :]

    ubuf[8:rows_step + 8, :] = qk_ref[...]

    t_idx = lax.broadcasted_iota(jnp.int32, (L, L), 0)
    s_idx = lax.broadcasted_iota(jnp.int32, (L, L), 1)
    causal = s_idx <= t_idx
    tril = jnp.where(causal, 1.0, 0.0).astype(BF16)
    col = lax.broadcasted_iota(jnp.int32, (L, LANES), 1)

    units = [(ci, h) for ci in range(nchunk) for h in range(H)]
    pre = []
    for ci in range(nchunk):
        r0 = ci * L
        y = jnp.broadcast_to(cb_ref[...], (L, W2))
        for j in range(CONV_W):
            lo = r0 + 8 - (CONV_W - 1) + j
            y = y + ubuf[lo:lo + L, :] * cw_ref[j:j + 1, :]
        qk = y * jax.nn.sigmoid(y)
        gates = gt_ref[r0:r0 + L, :] + bg_ref[...]
        gates = jnp.where(col >= H, _log_sigmoid(gates), gates)
        g_hi = gates.astype(BF16)
        g_lo = (gates - g_hi.astype(F32)).astype(BF16)
        csum = _dot(tril, g_hi) + _dot(tril, g_lo)
        pre.append((qk, gates, csum, gates.T, csum.T))

    def unit(ci, h):
        qk, gates, csum, gates_t, csum_t = pre[ci]
        sl = slice(h * MLSTM_DH, (h + 1) * MLSTM_DH)
        qh = qk[:, sl]
        kh = qk[:, MLSTM_W + h * MLSTM_DH:MLSTM_W + (h + 1) * MLSTM_DH] * (MLSTM_DH ** -0.5)
        b_row = csum_t[H + h:H + h + 1, :]
        return dict(qh=qh, kh=kh, qb=qh.astype(BF16), kb=kh.astype(BF16),
                    vb=v_ref[ci * L:(ci + 1) * L, sl].astype(BF16),
                    i_col=gates[:, h:h + 1], i_row=gates_t[h:h + 1, :],
                    b_col=csum[:, H + h:H + h + 1], b_row=b_row, bl=b_row[:, L - 1:L])

    U = {u: unit(*u) for u in units}
    for d in U.values():
        d['dmat'] = jnp.where(causal, d['b_col'] - d['b_row'] + d['i_row'], MASKED)
        d['qk'] = _dot_nt(d['qb'], d['kb'])
        d['g_col'] = d['bl'] - d['b_col'] + d['i_col']
    for d in U.values():
        d['m_loc'] = jnp.max(d['dmat'], axis=1, keepdims=True)
        d['g_max'] = jnp.max(d['g_col'], axis=0, keepdims=True)
    for d in U.values():
        d['sc'] = d['qk'] * jnp.exp(d['dmat'] - d['m_loc'])
        d['kw'] = jnp.exp(d['g_col'] - d['g_max']) * d['kh']
    for d in U.values():
        d['a_loc'] = _dot(d['sc'].astype(BF16), d['vb'])
        d['r_loc'] = jnp.sum(d['sc'], axis=1, keepdims=True)
        d['dc_loc'] = _dot_tn(d['kw'].astype(BF16), d['vb'])
        d['dn_loc'] = jnp.sum(d['kw'], axis=0, keepdims=True)

    c_st = [c_s[h] for h in range(H)]
    n_st = [n_s[h:h + 1, :] for h in range(H)]
    m_st = [m_s[h:h + 1, 0:1] for h in range(H)]
    for (ci, h), d in U.items():
        m_prev, c_prev, n_prev = m_st[h], c_st[h], n_st[h]
        a_col = d['b_col'] + m_prev
        mt = jnp.maximum(a_col, d['m_loc'])
        wa = jnp.exp(a_col - mt)
        wl = jnp.exp(d['m_loc'] - mt)
        num = wa * _dot(d['qb'], c_prev.astype(BF16)) + wl * d['a_loc']
        den = wa * jnp.sum(d['qh'] * n_prev, axis=1, keepdims=True) + wl * d['r_loc']
        d['hh'] = num / jnp.maximum(jnp.abs(den), jnp.exp(-mt))
        m_new = jnp.maximum(d['bl'] + m_prev, d['g_max'])
        wc = jnp.exp(d['bl'] + m_prev - m_new)
        wg = jnp.exp(d['g_max'] - m_new)
        c_st[h] = wc * c_prev + wg * d['dc_loc']
        n_st[h] = wc * n_prev + wg * d['dn_loc']
        m_st[h] = m_new
    for h in range(H):
        c_s[h] = c_st[h]
        n_s[h:h + 1, :] = n_st[h]
        m_s[h:h + 1, :] = jnp.broadcast_to(m_st[h], (1, LANES))

    for (ci, h), d in U.items():
        sl = slice(h * MLSTM_DH, (h + 1) * MLSTM_DH)
        hh = d['hh']
        mu = jnp.mean(hh, axis=1, keepdims=True)
        var = jnp.mean(jnp.square(hh - mu), axis=1, keepdims=True)
        hn = (hh - mu) * lax.rsqrt(var + EPS) * gmh_ref[:, sl]
        hm_ref[ci * L:(ci + 1) * L, sl] = hn * jax.nn.sigmoid(om_ref[ci * L:(ci + 1) * L, sl])

    @pl.when(c == last)
    def _():
        conv_ref[0] = ubuf[rows_step + 8 - (CONV_W - 1):rows_step + 8, :]
        c_out[0] = c_s[...]
        n_out[0] = n_s[...]
        m_out[0] = m_s[...]


def _mlstm_prompt(z, gates, conv_w, conv_b, bg_row, g_mh, *, batch, seq, chunk, nchunk):
    rows = chunk * nchunk
    nc = seq // rows
    n = batch * seq
    row = lambda b, c: b * nc + c
    W2 = 2 * MLSTM_W
    full = lambda shape: pl.BlockSpec(shape, lambda b, c: (0,) * len(shape))
    return pl.pallas_call(
        functools.partial(_mlstm_prompt_body, chunk=chunk, nchunk=nchunk),
        grid=(batch, nc),
        in_specs=[
            pl.BlockSpec((rows, W2), lambda b, c: (row(b, c), COL_QK // W2)),
            pl.BlockSpec((rows, MLSTM_W), lambda b, c: (row(b, c), COL_VM // MLSTM_W)),
            pl.BlockSpec((rows, MLSTM_W), lambda b, c: (row(b, c), COL_OM // MLSTM_W)),
            pl.BlockSpec((rows, LANES), lambda b, c: (row(b, c), 0)),
            full((CONV_W, W2)), full((1, W2)), full((1, LANES)), full((1, MLSTM_W)),
        ],
        out_specs=[
            pl.BlockSpec((rows, MLSTM_W), lambda b, c: (row(b, c), 0)),
            pl.BlockSpec((1, CONV_W - 1, W2), lambda b, c: (b, 0, 0)),
            pl.BlockSpec((1, MLSTM_HEADS, MLSTM_DH, MLSTM_DH), lambda b, c: (b, 0, 0, 0)),
            pl.BlockSpec((1, MLSTM_HEADS, MLSTM_DH), lambda b, c: (b, 0, 0)),
            pl.BlockSpec((1, MLSTM_HEADS, LANES), lambda b, c: (b, 0, 0)),
        ],
        out_shape=[
            jax.ShapeDtypeStruct((n, MLSTM_W), F32),
            jax.ShapeDtypeStruct((batch, CONV_W - 1, W2), F32),
            jax.ShapeDtypeStruct((batch, MLSTM_HEADS, MLSTM_DH, MLSTM_DH), F32),
            jax.ShapeDtypeStruct((batch, MLSTM_HEADS, MLSTM_DH), F32),
            jax.ShapeDtypeStruct((batch, MLSTM_HEADS, LANES), F32),
        ],
        scratch_shapes=[
            pltpu.VMEM((rows + 8, W2), F32),
            pltpu.VMEM((MLSTM_HEADS, MLSTM_DH, MLSTM_DH), F32),
            pltpu.VMEM((MLSTM_HEADS, MLSTM_DH), F32),
            pltpu.VMEM((MLSTM_HEADS, LANES), F32),
        ],
        compiler_params=_cparams(("arbitrary", "arbitrary"), 32),
        name="mlstm_prompt",
    )(z, z, z, gates, conv_w, conv_b, bg_row, g_mh)


def _dilated_prompt_body(q_ref, k_ref, kp_ref, v_ref, vp_ref, o_ref, kpad, vpad, acc, mm, ll, bias, *, tq):
    i = pl.program_id(2)
    hd = ATTN_DH
    dmax = max(d for _, d in DILATED)
    ngrp = tq // dmax
    half = ngrp * PAD_PITCH
    cur = lax.rem(i, 2) * half
    prev = lax.rem(i + 1, 2) * half

    @pl.when(i == 0)
    def _():
        kpad[pl.ds(pl.multiple_of(prev, 8), half), :] = jnp.zeros((half, LANES), F32)
        vpad[pl.ds(pl.multiple_of(prev, 8), half), :] = jnp.zeros((half, LANES), F32)

    def pad_copy(g, carry):
        src = pl.ds(pl.multiple_of(g * dmax, dmax), dmax)
        dst = pl.ds(pl.multiple_of(cur + g * PAD_PITCH, 8), dmax)
        kpad[dst, :] = k_ref[src, :]
        vpad[dst, :] = v_ref[src, :]
        return carry

    lax.fori_loop(0, ngrp, pad_copy, 0, unroll=8)

    lane = lax.broadcasted_iota(jnp.int32, (REL, LANES), 1)
    first_head = lane < hd
    a_idx = lax.broadcasted_iota(jnp.int32, (REL, 2 * REL), 0)
    j_idx = lax.broadcasted_iota(jnp.int32, (REL, 2 * REL), 1)
    band = (j_idx >= a_idx) & (j_idx <= a_idx + REL)
    b0 = jnp.where(band, 0.0, MASKED)
    b1 = jnp.where(band & (j_idx >= jnp.where(i == 0, REL, 0)), 0.0, MASKED)
    bias[0] = jnp.concatenate([b0, b0], axis=0)
    bias[1] = jnp.concatenate([b1, b1], axis=0)

    order = sorted(DILATED, key=lambda wd: -wd[1])
    for bi, (window, d) in enumerate(order):
        assert window // d == REL and tq % (REL * d) == 0
        nblk = tq // (REL * d)
        for r in range(d):
            for cb in range(nblk):
                rows = pl.ds(r + d * REL * cb, REL, stride=d)
                q2 = q_ref[rows, :] * (hd ** -0.5)
                if d == dmax:
                    kp = kpad[pl.ds(prev + r, REL, stride=PAD_PITCH), :]
                    kc = kpad[pl.ds(cur + r, REL, stride=PAD_PITCH), :]
                    vp = vpad[pl.ds(prev + r, REL, stride=PAD_PITCH), :]
                    vc = vpad[pl.ds(cur + r, REL, stride=PAD_PITCH), :]
                    k2 = jnp.concatenate([kp, kc], axis=0)
                    v2 = jnp.concatenate([vp, vc], axis=0)
                elif cb == 0:
                    tail = pl.ds(r + d * REL * (nblk - 1), REL, stride=d)
                    k2 = jnp.concatenate([kp_ref[tail, :], k_ref[rows, :]], axis=0)
                    v2 = jnp.concatenate([vp_ref[tail, :], v_ref[rows, :]], axis=0)
                else:
                    both = pl.ds(r + d * REL * (cb - 1), 2 * REL, stride=d)
                    k2 = k_ref[both, :]
                    v2 = v_ref[both, :]
                qs = jnp.concatenate([jnp.where(first_head, q2, 0.0), jnp.where(first_head, 0.0, q2)], axis=0)
                s = _dot_nt(qs.astype(BF16), k2.astype(BF16)) + bias[1 if cb == 0 else 0]
                mx = jnp.max(s, axis=1, keepdims=True)
                p = jnp.exp(s - mx)
                l = jnp.sum(p, axis=1, keepdims=True)
                o = _dot(p.astype(BF16), v2.astype(BF16))
                m_blk = jnp.where(first_head, mx[0:REL], mx[REL:2 * REL])
                l_blk = jnp.where(first_head, l[0:REL], l[REL:2 * REL])
                o_blk = jnp.where(first_head, o[0:REL], o[REL:2 * REL])
                if bi == 0:
                    mm[rows, :] = m_blk
                    ll[rows, :] = l_blk
                    acc[rows, :] = o_blk
                else:
                    m_old = mm[rows, :]
                    m_new = jnp.maximum(m_old, m_blk)
                    w_old = jnp.exp(m_old - m_new)
                    w_blk = jnp.exp(m_blk - m_new)
                    mm[rows, :] = m_new
                    ll[rows, :] = ll[rows, :] * w_old + l_blk * w_blk
                    acc[rows, :] = acc[rows, :] * w_old + o_blk * w_blk

    o_ref[...] = acc[...] / ll[...]


def _dilated_prompt(z, *, batch, seq, tq):
    nt = seq // tq
    n = batch * seq
    pairs = ATTN_W // LANES
    dmax = max(d for _, d in DILATED)
    assert tq == REL * dmax
    pad_rows = 2 * (tq // dmax) * PAD_PITCH

    def spec(col0, back):
        return pl.BlockSpec((tq, LANES),
                            lambda b, p, i: (b * nt + jnp.maximum(i - back, 0), col0 // LANES + p))

    return pl.pallas_call(
        functools.partial(_dilated_prompt_body, tq=tq),
        grid=(batch, pairs, nt),
        in_specs=[spec(COL_QA, 0), spec(COL_KA, 0), spec(COL_KA, 1), spec(COL_VA, 0), spec(COL_VA, 1)],
        out_specs=pl.BlockSpec((tq, LANES), lambda b, p, i: (b * nt + i, p)),
        out_shape=jax.ShapeDtypeStruct((n, ATTN_W), F32),
        scratch_shapes=[
            pltpu.VMEM((pad_rows, LANES), F32), pltpu.VMEM((pad_rows, LANES), F32),
            pltpu.VMEM((tq, LANES), F32), pltpu.VMEM((tq, LANES), F32), pltpu.VMEM((tq, LANES), F32),
            pltpu.VMEM((2, 2 * REL, 2 * REL), F32),
        ],
        compiler_params=_cparams(("arbitrary", "arbitrary", "arbitrary"), 32),
        name="dilated_prompt",
    )(z, z, z, z, z)


def _outproj_q_body(x_ref, hm_ref, ha_ref, wom_ref, woa_ref, gx_ref, wq_ref, x1_ref, q_ref):
    x1 = x_ref[...] + _dot(hm_ref[...].astype(BF16), wom_ref[...]) + _dot(ha_ref[...].astype(BF16), woa_ref[...])
    x1_ref[...] = x1
    q_ref[...] = _dot(_rms(x1, gx_ref[...]).astype(BF16), wq_ref[...]).astype(BF16)


def _outproj_q(x, hm, ha, w_out_m, w_out_a, g_x, w_mq, *, tm):
    n, d = x.shape
    row = lambda w: pl.BlockSpec((tm, w), lambda i: (i, 0))
    full = lambda shape: pl.BlockSpec(shape, lambda i: (0, 0))
    return pl.pallas_call(
        _outproj_q_body,
        grid=(n // tm,),
        in_specs=[row(d), row(MLSTM_W), row(ATTN_W), full((MLSTM_W, d)), full((ATTN_W, d)),
                  full((1, d)), full((d, d))],
        out_specs=[row(d), row(d)],
        out_shape=[jax.ShapeDtypeStruct((n, d), F32), jax.ShapeDtypeStruct((n, d), BF16)],
        compiler_params=_cparams(("arbitrary",), 40),
        name="outproj_q",
    )(x, hm, ha, w_out_m, w_out_a, g_x.reshape(1, d), w_mq)


def _xattn_prompt_body(q_ref, k_ref, v_ref, o_ref):
    for h in range(MEM_HEADS):
        sl = slice(h * MEM_DH, (h + 1) * MEM_DH)
        s = _dot_nt(q_ref[:, sl], k_ref[:, sl].astype(BF16)) * (MEM_DH ** -0.5)
        e = jnp.exp(s - jnp.max(s, axis=1, keepdims=True))
        p = e / jnp.sum(e, axis=1, keepdims=True)
        o_ref[:, sl] = _dot(p.astype(BF16), v_ref[:, sl].astype(BF16)).astype(BF16)


def _xattn_prompt(q, zmem, *, batch, seq, mem_len, tm):
    n, d = q.shape
    nt = seq // tm
    return pl.pallas_call(
        _xattn_prompt_body,
        grid=(batch, nt),
        in_specs=[
            pl.BlockSpec((tm, d), lambda b, i: (b * nt + i, 0)),
            pl.BlockSpec((mem_len, d), lambda b, i: (b, 0)),
            pl.BlockSpec((mem_len, d), lambda b, i: (b, 1)),
        ],
        out_specs=pl.BlockSpec((tm, d), lambda b, i: (b * nt + i, 0)),
        out_shape=jax.ShapeDtypeStruct((n, d), BF16),
        compiler_params=_cparams(("arbitrary", "arbitrary"), 32),
        name="xattn_prompt",
    )(q, zmem, zmem)


def _ffn_steps(x1_ref, o_ref, wmo_ref, gf_ref, wg_ref, wu_ref, wd_ref, gfin_ref, y_ref):
    x2 = x1_ref[...] + _dot(o_ref[...], wmo_ref[...])
    h = _rms(x2, gf_ref[...]).astype(BF16)
    yield
    g = _dot(h, wg_ref[...])
    yield
    u = _dot(h, wu_ref[...])
    yield
    a = (g * jax.nn.sigmoid(g) * u).astype(BF16)
    yield
    acc = x2 + _dot(a, wd_ref[...])
    yield
    y_ref[...] = _rms(acc, gfin_ref[...])


def _run(*gens):
    gens = list(gens)
    while gens:
        alive = []
        for g in gens:
            try:
                next(g)
                alive.append(g)
            except StopIteration:
                pass
        gens = alive


def _post_ffn_body(*refs):
    _run(_ffn_steps(*refs))


def _post_ffn(x1, o, w_mo, g_ffn, w_gate, w_up, w_down, g_final, *, tm):
    n, d = x1.shape
    dff = w_gate.shape[1]
    tm = min(tm, n)
    assert n % tm == 0
    row = lambda: pl.BlockSpec((tm, d), lambda i: (i, 0))
    full = lambda shape: pl.BlockSpec(shape, lambda i: (0, 0), pipeline_mode=pl.Buffered(1))
    weights = 2 * (d * d + 3 * d * dff)
    work = tm * (6 * d * 4 + 2 * dff * 4 + dff * 2 + d * 2)
    return pl.pallas_call(
        _post_ffn_body,
        grid=(n // tm,),
        in_specs=[row(), row(), full((d, d)), full((1, d)), full((d, dff)), full((d, dff)),
                  full((dff, d)), full((1, d))],
        out_specs=row(),
        out_shape=jax.ShapeDtypeStruct((n, d), F32),
        compiler_params=_cparams(("arbitrary",), (weights + work) // MIB + 8),
        name="post_ffn",
    )(x1, o, w_mo, g_ffn.reshape(1, d), w_gate, w_up, w_down, g_final.reshape(1, d))


def _mlstm_sample_steps(qk_ref, v_ref, om_ref, gt_ref, conv_ref, c_ref, n_ref, m_ref,
                        cw_ref, cb_ref, bg_ref, gmh_ref,
                        hm_ref, conv_out, c_out, n_out, m_out, *, steps):
    T = steps
    W2 = 2 * MLSTM_W
    b = 0
    t_idx = lax.broadcasted_iota(jnp.int32, (T, T), 0)
    s_idx = lax.broadcasted_iota(jnp.int32, (T, T), 1)
    causal = s_idx <= t_idx
    eye = s_idx == t_idx
    col = lax.broadcasted_iota(jnp.int32, (T, LANES), 1)
    pad = jnp.zeros((8 - T, MLSTM_DH), F32)

    def to_row(v_col):
        return jnp.sum(jnp.where(eye, v_col, 0.0), axis=0, keepdims=True)

    up = jnp.concatenate([conv_ref[b], qk_ref[b]], axis=0)
    y = jnp.broadcast_to(cb_ref[...], (T, W2))
    for j in range(CONV_W):
        y = y + up[j:j + T, :] * cw_ref[j:j + 1, :]
    conv_out[b] = up[T:T + CONV_W - 1, :]
    qk = y * jax.nn.sigmoid(y)
    gates = gt_ref[b] + bg_ref[...]
    gates = jnp.where(col >= MLSTM_HEADS, _log_sigmoid(gates), gates)
    v_all = v_ref[b]
    om_all = om_ref[b]
    m_all = m_ref[b]
    yield
    st = []
    for h in range(MLSTM_HEADS):
        sl = slice(h * MLSTM_DH, (h + 1) * MLSTM_DH)
        qh = qk[:, sl]
        kh = qk[:, MLSTM_W + h * MLSTM_DH:MLSTM_W + (h + 1) * MLSTM_DH] * (MLSTM_DH ** -0.5)
        vh = v_all[:, sl]
        i_col = gates[:, h:h + 1]
        f_col = gates[:, MLSTM_HEADS + h:MLSTM_HEADS + h + 1]
        c_prev = c_ref[b, h]
        q8 = jnp.concatenate([qh, pad], axis=0).astype(BF16)
        inter = _dot(q8, c_prev.astype(BF16))[0:T, :]
        st.append(dict(qh=qh, kh=kh, vh=vh, i_col=i_col, f_col=f_col, c_prev=c_prev, inter=inter,
                       i_row=to_row(i_col), f_row=to_row(f_col)))
    yield
    for h, d in enumerate(st):
        d['b_col'] = jnp.sum(jnp.where(causal, d['f_row'], 0.0), axis=1, keepdims=True)
        qkt = jnp.zeros((T, T), F32)
        for s in range(T):
            dots = jnp.sum(d['qh'] * d['kh'][s:s + 1, :], axis=1, keepdims=True)
            qkt = jnp.where(s_idx == s, dots, qkt)
        d['qkt'] = qkt
    yield
    for h, d in enumerate(st):
        m_prev = m_all[:, h:h + 1]
        n_prev = n_ref[b, h:h + 1, :]
        b_col = d['b_col']
        b_row = to_row(b_col)
        dmat = jnp.where(causal, b_col - b_row + d['i_row'], MASKED)
        a_col = b_col + m_prev
        mt = jnp.maximum(a_col, jnp.max(dmat, axis=1, keepdims=True))
        wa = jnp.exp(a_col - mt)
        sc = d['qkt'] * jnp.exp(dmat - mt)
        intra = jnp.zeros((T, MLSTM_DH), F32)
        for s in range(T):
            intra = intra + sc[:, s:s + 1] * d['vh'][s:s + 1, :]
        num = wa * d['inter'] + intra
        den = wa * jnp.sum(d['qh'] * n_prev, axis=1, keepdims=True) + jnp.sum(sc, axis=1, keepdims=True)
        d['hh'] = num / jnp.maximum(jnp.abs(den), jnp.exp(-mt))
        bl = b_col[T - 1:T, :]
        g_col = bl - b_col + d['i_col']
        m_new = jnp.maximum(bl + m_prev, jnp.max(g_col, axis=0, keepdims=True))
        d['wc'] = jnp.exp(bl + m_prev - m_new)
        kw = jnp.exp(g_col - m_new) * d['kh']
        kw8 = jnp.concatenate([kw, pad], axis=0).astype(BF16)
        v8 = jnp.concatenate([d['vh'], pad], axis=0).astype(BF16)
        d['dc'] = _dot_tn(kw8, v8)
        n_out[b, h:h + 1, :] = d['wc'] * n_prev + jnp.sum(kw, axis=0, keepdims=True)
        m_out[b, h:h + 1, :] = jnp.broadcast_to(m_new, (1, LANES))
    yield
    hs = []
    for h, d in enumerate(st):
        sl = slice(h * MLSTM_DH, (h + 1) * MLSTM_DH)
        c_out[b, h] = d['wc'] * d['c_prev'] + d['dc']
        hh = d['hh']
        mu = jnp.mean(hh, axis=1, keepdims=True)
        var = jnp.mean(jnp.square(hh - mu), axis=1, keepdims=True)
        hn = (hh - mu) * lax.rsqrt(var + EPS) * gmh_ref[:, sl]
        hs.append(hn * jax.nn.sigmoid(om_all[:, sl]))
    hm_ref[b] = jnp.concatenate(hs, axis=1)


def _branch_count(dist):
    cnt = jnp.zeros(dist.shape, F32)
    for window, d in DILATED:
        ok = (dist >= 0) & (dist <= window) & (lax.rem(jnp.maximum(dist, 0), d) == 0)
        cnt = cnt + jnp.where(ok, 1.0, 0.0)
    return cnt


def _dilated_sample_steps(q_ref, kn_ref, vn_ref, kc_ref, vc_ref, o_ref, ot_s, fresh_s, den_s, *, steps, wb):
    Q = q_ref.shape[2]
    t_c = lax.broadcasted_iota(jnp.int32, (Q, wb), 0)
    pos_c = lax.broadcasted_iota(jnp.int32, (Q, wb), 1)
    cnt_c = _branch_count(wb + t_c - pos_c)
    t_n = lax.broadcasted_iota(jnp.int32, (Q, Q), 0)
    s_n = lax.broadcasted_iota(jnp.int32, (Q, Q), 1)
    cnt_n = jnp.where(s_n < steps, _branch_count(t_n - s_n), 0.0)
    lane_t = lax.broadcasted_iota(jnp.int32, (ATTN_DH, LANES), 1)

    def head(h):
        qh = q_ref[0, h] * (ATTN_DH ** -0.5)
        knh = kn_ref[0, h]
        vnh = vn_ref[0, h]
        s_c = _dot(qh.astype(BF16), kc_ref[0, h].astype(BF16))
        s_new = jnp.zeros((Q, Q), F32)
        for s in range(steps):
            dots = jnp.sum(qh * knh[s:s + 1, :], axis=1, keepdims=True)
            s_new = jnp.where(s_n == s, dots, s_new)
        s_c = jnp.where(cnt_c > 0, s_c, MASKED)
        s_new = jnp.where(cnt_n > 0, s_new, MASKED)
        mx = jnp.maximum(jnp.max(s_c, axis=1, keepdims=True), jnp.max(s_new, axis=1, keepdims=True))
        p_c = cnt_c * jnp.exp(s_c - mx)
        p_n = cnt_n * jnp.exp(s_new - mx)
        den = jnp.sum(p_c, axis=1, keepdims=True) + jnp.sum(p_n, axis=1, keepdims=True)
        den_s[h] = jnp.broadcast_to(den, (Q, ATTN_DH))
        part = [jnp.zeros((ATTN_DH, LANES), F32) for _ in range(steps)]
        for c in range(wb // LANES):
            cs = slice(c * LANES, (c + 1) * LANES)
            v_c = vc_ref[0, h, :, cs]
            for t in range(steps):
                part[t] = part[t] + v_c * p_c[t:t + 1, cs]
        o_t = jnp.zeros((ATTN_DH, LANES), F32)
        for t in range(steps):
            o_t = jnp.where(lane_t == t, jnp.sum(part[t], axis=1, keepdims=True), o_t)
        ot_s[h * ATTN_DH:(h + 1) * ATTN_DH, :] = o_t
        o_new = jnp.zeros((Q, ATTN_DH), F32)
        for s in range(steps):
            o_new = o_new + p_n[:, s:s + 1] * vnh[s:s + 1, :]
        fresh_s[h] = o_new

    for h in range(ATTN_HEADS):
        head(h)
        if h % 2 == 1:
            yield
    o_cached = ot_s[...].T
    for h in range(ATTN_HEADS):
        o_ref[0, h] = (o_cached[0:Q, h * ATTN_DH:(h + 1) * ATTN_DH] + fresh_s[h]) / den_s[h]


N_FFN_IN, N_ATTN_IN, N_MLSTM_IN = 8, 5, 12


def _ffn_and_sample_mixers_body(*refs, steps, wb):
    ffn_in, refs = refs[:N_FFN_IN], refs[N_FFN_IN:]
    attn_in, refs = refs[:N_ATTN_IN], refs[N_ATTN_IN:]
    mlstm_in, refs = refs[:N_MLSTM_IN], refs[N_MLSTM_IN:]
    y_ref, oa_ref = refs[0], refs[1]
    mlstm_out, attn_scratch = refs[2:7], refs[7:]
    _run(_ffn_steps(*ffn_in, y_ref),
         _dilated_sample_steps(*attn_in, oa_ref, *attn_scratch, steps=steps, wb=wb),
         _mlstm_sample_steps(*mlstm_in, *mlstm_out, steps=steps))


def _ffn_and_sample_mixers(x1, o, w_mo, g_ffn, w_gate, w_up, w_down, g_final,
                           q8, kn8, vn8, kc_t, vc_t,
                           z3, gates3, conv, c0, n0, m0, conv_w, conv_b, bg_row, g_mh, *, steps):
    n, d = x1.shape
    dff = w_gate.shape[1]
    batch, _, qrows, _ = q8.shape
    wb = kc_t.shape[-1]
    W2 = 2 * MLSTM_W
    assert n % batch == 0
    tm = n // batch
    row = lambda: pl.BlockSpec((tm, d), lambda i: (i, 0))
    once = lambda shape: pl.BlockSpec(shape, lambda i: (0,) * len(shape), pipeline_mode=pl.Buffered(1))
    per_seq = lambda *shape: pl.BlockSpec((1,) + shape, lambda b: (b,) + (0,) * len(shape))
    zcols = lambda width, col0: pl.BlockSpec((1, steps, width), lambda b: (b, 0, col0 // width))
    new = lambda: per_seq(ATTN_HEADS, qrows, ATTN_DH)
    cache = lambda: per_seq(ATTN_HEADS, ATTN_DH, wb)
    weights = 2 * (d * d + 3 * d * dff)
    work = tm * (6 * d * 4 + 2 * dff * 4 + dff * 2 + d * 2)
    caches = 2 * 2 * ATTN_W * wb * 4
    state = 4 * MLSTM_HEADS * MLSTM_DH * MLSTM_DH * 4
    return pl.pallas_call(
        functools.partial(_ffn_and_sample_mixers_body, steps=steps, wb=wb),
        grid=(batch,),
        in_specs=[row(), row(), once((d, d)), once((1, d)), once((d, dff)), once((d, dff)),
                  once((dff, d)), once((1, d)),
                  new(), new(), new(), cache(), cache(),
                  zcols(W2, COL_QK), zcols(MLSTM_W, COL_VM), zcols(MLSTM_W, COL_OM),
                  per_seq(steps, LANES), per_seq(CONV_W - 1, W2),
                  per_seq(MLSTM_HEADS, MLSTM_DH, MLSTM_DH), per_seq(MLSTM_HEADS, MLSTM_DH),
                  per_seq(1, LANES),
                  once((CONV_W, W2)), once((1, W2)), once((1, LANES)), once((1, MLSTM_W))],
        out_specs=[row(), new(),
                   per_seq(steps, MLSTM_W), per_seq(CONV_W - 1, W2),
                   per_seq(MLSTM_HEADS, MLSTM_DH, MLSTM_DH), per_seq(MLSTM_HEADS, MLSTM_DH),
                   per_seq(MLSTM_HEADS, LANES)],
        out_shape=[jax.ShapeDtypeStruct((n, d), F32),
                   jax.ShapeDtypeStruct((batch, ATTN_HEADS, qrows, ATTN_DH), F32),
                   jax.ShapeDtypeStruct((batch, steps, MLSTM_W), F32),
                   jax.ShapeDtypeStruct((batch, CONV_W - 1, W2), F32),
                   jax.ShapeDtypeStruct((batch, MLSTM_HEADS, MLSTM_DH, MLSTM_DH), F32),
                   jax.ShapeDtypeStruct((batch, MLSTM_HEADS, MLSTM_DH), F32),
                   jax.ShapeDtypeStruct((batch, MLSTM_HEADS, LANES), F32)],
        scratch_shapes=[pltpu.VMEM((ATTN_W, LANES), F32),
                        pltpu.VMEM((ATTN_HEADS, qrows, ATTN_DH), F32),
                        pltpu.VMEM((ATTN_HEADS, qrows, ATTN_DH), F32)],
        compiler_params=_cparams(("arbitrary",), (weights + work + caches + state) // MIB + 10),
        name="ffn_and_sample_mixers",
    )(x1, o, w_mo, g_ffn.reshape(1, d), w_gate, w_up, w_down, g_final.reshape(1, d),
      q8, kn8, vn8, kc_t, vc_t,
      z3, z3, z3, gates3, conv, c0, n0, m0, conv_w, conv_b, bg_row, g_mh)


def _xattn_sample_body(q_ref, k_ref, v_ref, o_ref):
    Q = q_ref.shape[1]
    mem_len = k_ref.shape[1]
    rows = mem_len * MEM_HEADS
    for g in range(q_ref.shape[0]):
        k2 = k_ref[g].reshape(rows, MEM_DH).astype(BF16)
        v2 = v_ref[g].reshape(rows, MEM_DH).astype(BF16)
        q_all = q_ref[g]
        qs = jnp.concatenate([q_all[:, h * MEM_DH:(h + 1) * MEM_DH] for h in range(MEM_HEADS)], axis=0)
        s = _dot_nt(qs, k2) * (MEM_DH ** -0.5)
        q_head = lax.div(lax.broadcasted_iota(jnp.int32, s.shape, 0), Q)
        k_head = lax.rem(lax.broadcasted_iota(jnp.int32, s.shape, 1), MEM_HEADS)
        s = jnp.where(q_head == k_head, s, MASKED)
        e = jnp.exp(s - jnp.max(s, axis=1, keepdims=True))
        p = e / jnp.sum(e, axis=1, keepdims=True)
        o = _dot(p.astype(BF16), v2)
        for h in range(MEM_HEADS):
            o_ref[g, :, h * MEM_DH:(h + 1) * MEM_DH] = o[h * Q:(h + 1) * Q, :].astype(BF16)


def _xattn_sample(q8, mem_k, mem_v, *, group):
    batch, qrows, d = q8.shape
    mem_len = mem_k.shape[1]
    mem = lambda: pl.BlockSpec((group, mem_len, MEM_HEADS, MEM_DH), lambda b: (b, 0, 0, 0))
    return pl.pallas_call(
        _xattn_sample_body,
        grid=(batch // group,),
        in_specs=[pl.BlockSpec((group, qrows, d), lambda b: (b, 0, 0)), mem(), mem()],
        out_specs=pl.BlockSpec((group, qrows, d), lambda b: (b, 0, 0)),
        out_shape=jax.ShapeDtypeStruct((batch, qrows, d), BF16),
        compiler_params=_cparams(("arbitrary",), 32),
        name="xattn_sample",
    )(q8, mem_k, mem_v)


def _pad_rows(a, rows):
    return jnp.pad(a, ((0, 0), (0, rows - a.shape[1]), (0, 0)))


def kernel(x_prompt, x_sample, mem_prompt, cache_attn_k, cache_attn_v, cache_mem_k, cache_mem_v, state_conv, state_C, state_n, state_m, g_mix, w_in, conv_w, conv_b, b_gates, g_mh, w_out, g_mem, w_mk, w_mv, g_xattn, w_mq, w_mo, g_ffn, w_gate, w_up, w_down, g_final):
    B, T, D = x_prompt.shape
    SB, ST, _ = x_sample.shape
    mem_len = mem_prompt.shape[1]
    W2 = 2 * MLSTM_W
    gate0 = COL_QA
    n_gate = 2 * MLSTM_HEADS

    w_main = jnp.concatenate([w_in[:, :gate0], w_in[:, gate0 + n_gate:]], axis=1).astype(BF16)
    w_gcols = jnp.pad(w_in[:, gate0:gate0 + n_gate], ((0, 0), (0, LANES - n_gate))).astype(BF16)
    bg_row = jnp.pad(b_gates, (0, LANES - n_gate)).reshape(1, LANES)
    w_mem = jnp.concatenate([w_mk, w_mv], axis=1).astype(BF16)
    w_out_m = w_out[:MLSTM_W].astype(BF16)
    w_out_a = w_out[MLSTM_W:].astype(BF16)
    w_mq_b, w_mo_b = w_mq.astype(BF16), w_mo.astype(BF16)
    w_gate_b, w_up_b, w_down_b = w_gate.astype(BF16), w_up.astype(BF16), w_down.astype(BF16)
    conv_b2 = conv_b.reshape(1, W2)
    g_mh2 = g_mh.reshape(1, MLSTM_W)

    xp = x_prompt.reshape(B * T, D)
    zmem = _norm_matmul(mem_prompt.reshape(B * mem_len, D), g_mem, w_mem, tm=B * mem_len, tn=512)
    z, gates = _norm_matmul(xp, g_mix, w_main, w_gcols, tm=2048, tn=512)
    hm, p_conv, p_C, p_n, p_m = _mlstm_prompt(z, gates, conv_w, conv_b2, bg_row, g_mh2,
                                              batch=B, seq=T, chunk=128, nchunk=4)
    ha = _dilated_prompt(z, batch=B, seq=T, tq=2048)
    x1, q = _outproj_q(xp, hm, ha, w_out_m, w_out_a, g_xattn, w_mq_b, tm=512)
    o = _xattn_prompt(q, zmem, batch=B, seq=T, mem_len=mem_len, tm=512)

    keep = min(DILATED[-1][0], T)
    z3p = z.reshape(B, T, N_MAIN)
    p_attn_k = z3p[:, T - keep:, COL_KA:COL_KA + ATTN_W].reshape(B, keep, ATTN_HEADS, ATTN_DH)
    p_attn_v = z3p[:, T - keep:, COL_VA:COL_VA + ATTN_W].reshape(B, keep, ATTN_HEADS, ATTN_DH)
    p_mem_k = zmem[:, :D].reshape(B, mem_len, MEM_HEADS, MEM_DH)
    p_mem_v = zmem[:, D:].reshape(B, mem_len, MEM_HEADS, MEM_DH)

    xs = x_sample.reshape(SB * ST, D)
    zs, gates_s = _norm_matmul(xs, g_mix, w_main, w_gcols, tm=SB * ST, tn=512)
    zs3 = zs.reshape(SB, ST, N_MAIN)
    m0 = jnp.pad(state_m, ((0, 0), (0, LANES - MLSTM_HEADS))).reshape(SB, 1, LANES)
    qrows = 8

    def heads_first(col0):
        a = zs3[:, :, col0:col0 + ATTN_W].reshape(SB, ST, ATTN_HEADS, ATTN_DH).transpose(0, 2, 1, 3)
        return jnp.pad(a, ((0, 0), (0, 0), (0, qrows - ST), (0, 0)))

    y_prompt, ha_s, hm_s, s_conv, s_C, s_n, s_m = _ffn_and_sample_mixers(
        x1, o, w_mo_b, g_ffn, w_gate_b, w_up_b, w_down_b, g_final,
        heads_first(COL_QA), heads_first(COL_KA), heads_first(COL_VA),
        jnp.transpose(cache_attn_k, (0, 2, 3, 1)), jnp.transpose(cache_attn_v, (0, 2, 3, 1)),
        zs3, gates_s.reshape(SB, ST, LANES), state_conv, state_C, state_n, m0,
        conv_w, conv_b2, bg_row, g_mh2, steps=ST)
    ha_s = ha_s[:, :, :ST].transpose(0, 2, 1, 3)
    x1s, qs = _outproj_q(xs, hm_s.reshape(SB * ST, MLSTM_W), ha_s.reshape(SB * ST, ATTN_W),
                         w_out_m, w_out_a, g_xattn, w_mq_b, tm=SB * ST)
    os_ = _xattn_sample(_pad_rows(qs.reshape(SB, ST, D), qrows), cache_mem_k, cache_mem_v, group=4)[:, :ST]
    y_sample = _post_ffn(x1s, os_.reshape(SB * ST, D), w_mo_b, g_ffn, w_gate_b, w_up_b, w_down_b,
                         g_final, tm=512)

    s_attn_k = zs3[:, :, COL_KA:COL_KA + ATTN_W].reshape(SB, ST, ATTN_HEADS, ATTN_DH)
    s_attn_v = zs3[:, :, COL_VA:COL_VA + ATTN_W].reshape(SB, ST, ATTN_HEADS, ATTN_DH)

    return (y_prompt.reshape(B, T, D), y_sample.reshape(SB, ST, D), p_attn_k, p_attn_v,
            p_conv, p_C, p_n, p_m[:, :, 0], p_mem_k, p_mem_v,
            s_attn_k, s_attn_v, s_conv, s_C, s_n, s_m[:, :, 0])
```

```python
import functools

import jax
import jax.numpy as jnp
from jax import lax
from jax.experimental import pallas as pl
from jax.experimental.pallas import tpu as pltpu

F32 = jnp.float32
BF16 = jnp.bfloat16

EPS = 1e-6
MASKED = -1e30

MLSTM_HEADS = 4
MLSTM_DH = 128
MLSTM_W = MLSTM_HEADS * MLSTM_DH
ATTN_HEADS = 8
ATTN_DH = 64
ATTN_W = ATTN_HEADS * ATTN_DH
DILATED = ((128, 1), (512, 4), (2048, 16))
REL = 128
CONV_W = 4
MEM_HEADS = 4
MEM_DH = 256
LANES = 128
MIB = 1024 * 1024
PAD_PITCH = 24

COL_QK = 0
COL_VM = 2 * MLSTM_W
COL_OM = 3 * MLSTM_W
COL_QA = 4 * MLSTM_W
COL_KA = COL_QA + ATTN_W
COL_VA = COL_KA + ATTN_W
N_MAIN = COL_VA + ATTN_W


def _cparams(semantics, vmem_mib):
    return pltpu.CompilerParams(dimension_semantics=semantics, vmem_limit_bytes=vmem_mib * MIB)


def _rms(x, g):
    return x * lax.rsqrt(jnp.mean(x * x, axis=-1, keepdims=True) + EPS) * g


def _log_sigmoid(x):
    return jnp.minimum(x, 0.0) - jnp.log1p(jnp.exp(-jnp.abs(x)))


def _dot(a, b):
    return jnp.dot(a, b, preferred_element_type=F32)


def _dot_nt(a, b):
    return lax.dot_general(a, b, (((1,), (1,)), ((), ())), preferred_element_type=F32)


def _dot_tn(a, b):
    return lax.dot_general(a, b, (((0,), (0,)), ((), ())), preferred_element_type=F32)


def _norm_matmul_body(x_ref, g_ref, w_ref, *rest, with_side):
    if with_side:
        ws_ref, z_ref, side_ref, h_ref = rest
    else:
        z_ref, h_ref = rest

    @pl.when(pl.program_id(1) == 0)
    def _():
        h = _rms(x_ref[...], g_ref[...]).astype(BF16)
        h_ref[...] = h
        if with_side:
            side_ref[...] = _dot(h, ws_ref[...])

    z_ref[...] = _dot(h_ref[...], w_ref[...])


def _norm_matmul(x, g, w, w_side=None, *, tm, tn):
    n, d = x.shape
    c = w.shape[1]
    with_side = w_side is not None
    in_specs = [
        pl.BlockSpec((tm, d), lambda i, j: (i, 0)),
        pl.BlockSpec((1, d), lambda i, j: (0, 0)),
        pl.BlockSpec((d, tn), lambda i, j: (0, j)),
    ]
    out_specs = [pl.BlockSpec((tm, tn), lambda i, j: (i, j))]
    out_shape = [jax.ShapeDtypeStruct((n, c), F32)]
    args = [x, g.reshape(1, d), w]
    if with_side:
        in_specs.append(pl.BlockSpec((d, LANES), lambda i, j: (0, 0)))
        out_specs.append(pl.BlockSpec((tm, LANES), lambda i, j: (i, 0)))
        out_shape.append(jax.ShapeDtypeStruct((n, LANES), F32))
        args.append(w_side)
    vmem = 2 * tm * d * 4 + tm * d * 2 + 2 * d * tn * 2 + 2 * tm * tn * 4 + 4 * tm * LANES * 4
    res = pl.pallas_call(
        functools.partial(_norm_matmul_body, with_side=with_side),
        grid=(n // tm, c // tn),
        in_specs=in_specs,
        out_specs=out_specs,
        out_shape=out_shape,
        scratch_shapes=[pltpu.VMEM((tm, d), BF16)],
        compiler_params=_cparams(("arbitrary", "arbitrary"), vmem // MIB + 8),
        name="norm_matmul",
    )(*args)
    return res if with_side else res[0]


def _mlstm_prompt_body(qk_ref, v_ref, om_ref, gt_ref, cw_ref, cb_ref, bg_ref, gmh_ref,
                       hm_ref, conv_ref, c_out, n_out, m_out,
                       ubuf, c_s, n_s, m_s, *, chunk, nchunk):
    L = chunk
    H = MLSTM_HEADS
    rows_step = L * nchunk
    c = pl.program_id(1)
    last = pl.num_programs(1) - 1
    W2 = 2 * MLSTM_W

    @pl.when(c == 0)
    def _():
        ubuf[0:8, :] = jnp.zeros((8, W2), F32)
        c_s[...] = jnp.zeros_like(c_s)
        n_s[...] = jnp.zeros_like(n_s)
        m_s[...] = jnp.zeros_like(m_s)

    @pl.when(c > 0)
    def _():
        ubuf[0:8, :] = ubuf[rows_step:rows_step + 8, :]

    ubuf[8:rows_step + 8, :] = qk_ref[...]

    t_idx = lax.broadcasted_iota(jnp.int32, (L, L), 0)
    s_idx = lax.broadcasted_iota(jnp.int32, (L, L), 1)
    causal = s_idx <= t_idx
    tril = jnp.where(causal, 1.0, 0.0).astype(BF16)
    col = lax.broadcasted_iota(jnp.int32, (L, LANES), 1)

    units = [(ci, h) for ci in range(nchunk) for h in range(H)]
    pre = []
    for ci in range(nchunk):
        r0 = ci * L
        y = jnp.broadcast_to(cb_ref[...], (L, W2))
        for j in range(CONV_W):
            lo = r0 + 8 - (CONV_W - 1) + j
            y = y + ubuf[lo:lo + L, :] * cw_ref[j:j + 1, :]
        qk = y * jax.nn.sigmoid(y)
        gates = gt_ref[r0:r0 + L, :] + bg_ref[...]
        gates = jnp.where(col >= H, _log_sigmoid(gates), gates)
        g_hi = gates.astype(BF16)
        g_lo = (gates - g_hi.astype(F32)).astype(BF16)
        csum = _dot(tril, g_hi) + _dot(tril, g_lo)
        pre.append((qk, gates, csum, gates.T, csum.T))

    def unit(ci, h):
        qk, gates, csum, gates_t, csum_t = pre[ci]
        sl = slice(h * MLSTM_DH, (h + 1) * MLSTM_DH)
        qh = qk[:, sl]
        kh = qk[:, MLSTM_W + h * MLSTM_DH:MLSTM_W + (h + 1) * MLSTM_DH] * (MLSTM_DH ** -0.5)
        b_row = csum_t[H + h:H + h + 1, :]
        return dict(qh=qh, kh=kh, qb=qh.astype(BF16), kb=kh.astype(BF16),
                    vb=v_ref[ci * L:(ci + 1) * L, sl].astype(BF16),
                    i_col=gates[:, h:h + 1], i_row=gates_t[h:h + 1, :],
                    b_col=csum[:, H + h:H + h + 1], b_row=b_row, bl=b_row[:, L - 1:L])

    U = {u: unit(*u) for u in units}
    for d in U.values():
        d['dmat'] = jnp.where(causal, d['b_col'] - d['b_row'] + d['i_row'], MASKED)
        d['qk'] = _dot_nt(d['qb'], d['kb'])
        d['g_col'] = d['bl'] - d['b_col'] + d['i_col']
    for d in U.values():
        d['m_loc'] = jnp.max(d['dmat'], axis=1, keepdims=True)
        d['g_max'] = jnp.max(d['g_col'], axis=0, keepdims=True)
    for d in U.values():
        d['sc'] = d['qk'] * jnp.exp(d['dmat'] - d['m_loc'])
        d['kw'] = jnp.exp(d['g_col'] - d['g_max']) * d['kh']
    for d in U.values():
        d['a_loc'] = _dot(d['sc'].astype(BF16), d['vb'])
        d['r_loc'] = jnp.sum(d['sc'], axis=1, keepdims=True)
        d['dc_loc'] = _dot_tn(d['kw'].astype(BF16), d['vb'])
        d['dn_loc'] = jnp.sum(d['kw'], axis=0, keepdims=True)

    c_st = [c_s[h] for h in range(H)]
    n_st = [n_s[h:h + 1, :] for h in range(H)]
    m_st = [m_s[h:h + 1, 0:1] for h in range(H)]
    for (ci, h), d in U.items():
        m_prev, c_prev, n_prev = m_st[h], c_st[h], n_st[h]
        a_col = d['b_col'] + m_prev
        mt = jnp.maximum(a_col, d['m_loc'])
        wa = jnp.exp(a_col - mt)
        wl = jnp.exp(d['m_loc'] - mt)
        num = wa * _dot(d['qb'], c_prev.astype(BF16)) + wl * d['a_loc']
        den = wa * jnp.sum(d['qh'] * n_prev, axis=1, keepdims=True) + wl * d['r_loc']
        d['hh'] = num / jnp.maximum(jnp.abs(den), jnp.exp(-mt))
        m_new = jnp.maximum(d['bl'] + m_prev, d['g_max'])
        wc = jnp.exp(d['bl'] + m_prev - m_new)
        wg = jnp.exp(d['g_max'] - m_new)
        c_st[h] = wc * c_prev + wg * d['dc_loc']
        n_st[h] = wc * n_prev + wg * d['dn_loc']
        m_st[h] = m_new
    for h in range(H):
        c_s[h] = c_st[h]
        n_s[h:h + 1, :] = n_st[h]
        m_s[h:h + 1, :] = jnp.broadcast_to(m_st[h], (1, LANES))

    for (ci, h), d in U.items():
        sl = slice(h * MLSTM_DH, (h + 1) * MLSTM_DH)
        hh = d['hh']
        mu = jnp.mean(hh, axis=1, keepdims=True)
        var = jnp.mean(jnp.square(hh - mu), axis=1, keepdims=True)
        hn = (hh - mu) * lax.rsqrt(var + EPS) * gmh_ref[:, sl]
        hm_ref[ci * L:(ci + 1) * L, sl] = hn * jax.nn.sigmoid(om_ref[ci * L:(ci + 1) * L, sl])

    @pl.when(c == last)
    def _():
        conv_ref[0] = ubuf[rows_step + 8 - (CONV_W - 1):rows_step + 8, :]
        c_out[0] = c_s[...]
        n_out[0] = n_s[...]
        m_out[0] = m_s[...]


def _mlstm_prompt(z, gates, conv_w, conv_b, bg_row, g_mh, *, batch, seq, chunk, nchunk):
    rows = chunk * nchunk
    nc = seq // rows
    n = batch * seq
    row = lambda b, c: b * nc + c
    W2 = 2 * MLSTM_W
    full = lambda shape: pl.BlockSpec(shape, lambda b, c: (0,) * len(shape))
    return pl.pallas_call(
        functools.partial(_mlstm_prompt_body, chunk=chunk, nchunk=nchunk),
        grid=(batch, nc),
        in_specs=[
            pl.BlockSpec((rows, W2), lambda b, c: (row(b, c), COL_QK // W2)),
            pl.BlockSpec((rows, MLSTM_W), lambda b, c: (row(b, c), COL_VM // MLSTM_W)),
            pl.BlockSpec((rows, MLSTM_W), lambda b, c: (row(b, c), COL_OM // MLSTM_W)),
            pl.BlockSpec((rows, LANES), lambda b, c: (row(b, c), 0)),
            full((CONV_W, W2)), full((1, W2)), full((1, LANES)), full((1, MLSTM_W)),
        ],
        out_specs=[
            pl.BlockSpec((rows, MLSTM_W), lambda b, c: (row(b, c), 0)),
            pl.BlockSpec((1, CONV_W - 1, W2), lambda b, c: (b, 0, 0)),
            pl.BlockSpec((1, MLSTM_HEADS, MLSTM_DH, MLSTM_DH), lambda b, c: (b, 0, 0, 0)),
            pl.BlockSpec((1, MLSTM_HEADS, MLSTM_DH), lambda b, c: (b, 0, 0)),
            pl.BlockSpec((1, MLSTM_HEADS, LANES), lambda b, c: (b, 0, 0)),
        ],
        out_shape=[
            jax.ShapeDtypeStruct((n, MLSTM_W), F32),
            jax.ShapeDtypeStruct((batch, CONV_W - 1, W2), F32),
            jax.ShapeDtypeStruct((batch, MLSTM_HEADS, MLSTM_DH, MLSTM_DH), F32),
            jax.ShapeDtypeStruct((batch, MLSTM_HEADS, MLSTM_DH), F32),
            jax.ShapeDtypeStruct((batch, MLSTM_HEADS, LANES), F32),
        ],
        scratch_shapes=[
            pltpu.VMEM((rows + 8, W2), F32),
            pltpu.VMEM((MLSTM_HEADS, MLSTM_DH, MLSTM_DH), F32),
            pltpu.VMEM((MLSTM_HEADS, MLSTM_DH), F32),
            pltpu.VMEM((MLSTM_HEADS, LANES), F32),
        ],
        compiler_params=_cparams(("arbitrary", "arbitrary"), 32),
        name="mlstm_prompt",
    )(z, z, z, gates, conv_w, conv_b, bg_row, g_mh)


def _dilated_prompt_body(q_ref, k_ref, kp_ref, v_ref, vp_ref, o_ref, kpad, vpad, acc, mm, ll, bias, *, tq):
    i = pl.program_id(2)
    hd = ATTN_DH
    dmax = max(d for _, d in DILATED)
    ngrp = tq // dmax
    half = ngrp * PAD_PITCH
    cur = lax.rem(i, 2) * half
    prev = lax.rem(i + 1, 2) * half

    @pl.when(i == 0)
    def _():
        kpad[pl.ds(pl.multiple_of(prev, 8), half), :] = jnp.zeros((half, LANES), F32)
        vpad[pl.ds(pl.multiple_of(prev, 8), half), :] = jnp.zeros((half, LANES), F32)

    def pad_copy(g, carry):
        src = pl.ds(pl.multiple_of(g * dmax, dmax), dmax)
        dst = pl.ds(pl.multiple_of(cur + g * PAD_PITCH, 8), dmax)
        kpad[dst, :] = k_ref[src, :]
        vpad[dst, :] = v_ref[src, :]
        return carry

    lax.fori_loop(0, ngrp, pad_copy, 0, unroll=8)

    lane = lax.broadcasted_iota(jnp.int32, (REL, LANES), 1)
    first_head = lane < hd
    a_idx = lax.broadcasted_iota(jnp.int32, (REL, 2 * REL), 0)
    j_idx = lax.broadcasted_iota(jnp.int32, (REL, 2 * REL), 1)
    band = (j_idx >= a_idx) & (j_idx <= a_idx + REL)
    b0 = jnp.where(band, 0.0, MASKED)
    b1 = jnp.where(band & (j_idx >= jnp.where(i == 0, REL, 0)), 0.0, MASKED)
    bias[0] = jnp.concatenate([b0, b0], axis=0)
    bias[1] = jnp.concatenate([b1, b1], axis=0)

    order = sorted(DILATED, key=lambda wd: -wd[1])
    for bi, (window, d) in enumerate(order):
        assert window // d == REL and tq % (REL * d) == 0
        nblk = tq // (REL * d)
        for r in range(d):
            for cb in range(nblk):
                rows = pl.ds(r + d * REL * cb, REL, stride=d)
                q2 = q_ref[rows, :] * (hd ** -0.5)
                if d == dmax:
                    kp = kpad[pl.ds(prev + r, REL, stride=PAD_PITCH), :]
                    kc = kpad[pl.ds(cur + r, REL, stride=PAD_PITCH), :]
                    vp = vpad[pl.ds(prev + r, REL, stride=PAD_PITCH), :]
                    vc = vpad[pl.ds(cur + r, REL, stride=PAD_PITCH), :]
                    k2 = jnp.concatenate([kp, kc], axis=0)
                    v2 = jnp.concatenate([vp, vc], axis=0)
                elif cb == 0:
                    tail = pl.ds(r + d * REL * (nblk - 1), REL, stride=d)
                    k2 = jnp.concatenate([kp_ref[tail, :], k_ref[rows, :]], axis=0)
                    v2 = jnp.concatenate([vp_ref[tail, :], v_ref[rows, :]], axis=0)
                else:
                    both = pl.ds(r + d * REL * (cb - 1), 2 * REL, stride=d)
                    k2 = k_ref[both, :]
                    v2 = v_ref[both, :]
                qs = jnp.concatenate([jnp.where(first_head, q2, 0.0), jnp.where(first_head, 0.0, q2)], axis=0)
                s = _dot_nt(qs.astype(BF16), k2.astype(BF16)) + bias[1 if cb == 0 else 0]
                mx = jnp.max(s, axis=1, keepdims=True)
                p = jnp.exp(s - mx)
                l = jnp.sum(p, axis=1, keepdims=True)
                o = _dot(p.astype(BF16), v2.astype(BF16))
                m_blk = jnp.where(first_head, mx[0:REL], mx[REL:2 * REL])
                l_blk = jnp.where(first_head, l[0:REL], l[REL:2 * REL])
                o_blk = jnp.where(first_head, o[0:REL], o[REL:2 * REL])
                if bi == 0:
                    mm[rows, :] = m_blk
                    ll[rows, :] = l_blk
                    acc[rows, :] = o_blk
                else:
                    m_old = mm[rows, :]
                    m_new = jnp.maximum(m_old, m_blk)
                    w_old = jnp.exp(m_old - m_new)
                    w_blk = jnp.exp(m_blk - m_new)
                    mm[rows, :] = m_new
                    ll[rows, :] = ll[rows, :] * w_old + l_blk * w_blk
                    acc[rows, :] = acc[rows, :] * w_old + o_blk * w_blk

    o_ref[...] = acc[...] / ll[...]


def _dilated_prompt(z, *, batch, seq, tq):
    nt = seq // tq
    n = batch * seq
    pairs = ATTN_W // LANES
    dmax = max(d for _, d in DILATED)
    assert tq == REL * dmax
    pad_rows = 2 * (tq // dmax) * PAD_PITCH

    def spec(col0, back):
        return pl.BlockSpec((tq, LANES),
                            lambda b, p, i: (b * nt + jnp.maximum(i - back, 0), col0 // LANES + p))

    return pl.pallas_call(
        functools.partial(_dilated_prompt_body, tq=tq),
        grid=(batch, pairs, nt),
        in_specs=[spec(COL_QA, 0), spec(COL_KA, 0), spec(COL_KA, 1), spec(COL_VA, 0), spec(COL_VA, 1)],
        out_specs=pl.BlockSpec((tq, LANES), lambda b, p, i: (b * nt + i, p)),
        out_shape=jax.ShapeDtypeStruct((n, ATTN_W), F32),
        scratch_shapes=[
            pltpu.VMEM((pad_rows, LANES), F32), pltpu.VMEM((pad_rows, LANES), F32),
            pltpu.VMEM((tq, LANES), F32), pltpu.VMEM((tq, LANES), F32), pltpu.VMEM((tq, LANES), F32),
            pltpu.VMEM((2, 2 * REL, 2 * REL), F32),
        ],
        compiler_params=_cparams(("arbitrary", "arbitrary", "arbitrary"), 32),
        name="dilated_prompt",
    )(z, z, z, z, z)


def _outproj_q_body(x_ref, hm_ref, ha_ref, wom_ref, woa_ref, gx_ref, wq_ref, x1_ref, q_ref):
    x1 = x_ref[...] + _dot(hm_ref[...].astype(BF16), wom_ref[...]) + _dot(ha_ref[...].astype(BF16), woa_ref[...])
    x1_ref[...] = x1
    q_ref[...] = _dot(_rms(x1, gx_ref[...]).astype(BF16), wq_ref[...]).astype(BF16)


def _outproj_q(x, hm, ha, w_out_m, w_out_a, g_x, w_mq, *, tm):
    n, d = x.shape
    row = lambda w: pl.BlockSpec((tm, w), lambda i: (i, 0))
    full = lambda shape: pl.BlockSpec(shape, lambda i: (0, 0))
    return pl.pallas_call(
        _outproj_q_body,
        grid=(n // tm,),
        in_specs=[row(d), row(MLSTM_W), row(ATTN_W), full((MLSTM_W, d)), full((ATTN_W, d)),
                  full((1, d)), full((d, d))],
        out_specs=[row(d), row(d)],
        out_shape=[jax.ShapeDtypeStruct((n, d), F32), jax.ShapeDtypeStruct((n, d), BF16)],
        compiler_params=_cparams(("arbitrary",), 40),
        name="outproj_q",
    )(x, hm, ha, w_out_m, w_out_a, g_x.reshape(1, d), w_mq)


def _outproj_xattn_body(x_ref, hm_ref, ha_ref, wom_ref, woa_ref, gx_ref, wq_ref, k_ref, v_ref,
                        x1_ref, o_ref):
    x1 = x_ref[...] + _dot(hm_ref[...].astype(BF16), wom_ref[...]) + _dot(ha_ref[...].astype(BF16), woa_ref[...])
    x1_ref[...] = x1
    q = _dot(_rms(x1, gx_ref[...]).astype(BF16), wq_ref[...]).astype(BF16)
    for h in range(MEM_HEADS):
        sl = slice(h * MEM_DH, (h + 1) * MEM_DH)
        s = _dot_nt(q[:, sl], k_ref[:, sl].astype(BF16)) * (MEM_DH ** -0.5)
        e = jnp.exp(s - jnp.max(s, axis=1, keepdims=True))
        p = e / jnp.sum(e, axis=1, keepdims=True)
        o_ref[:, sl] = _dot(p.astype(BF16), v_ref[:, sl].astype(BF16)).astype(BF16)


def _outproj_xattn(x, hm, ha, w_out_m, w_out_a, g_x, w_mq, zmem, *, seq, mem_len, tm):
    n, d = x.shape
    assert seq % tm == 0
    tiles_per_seq = seq // tm
    row = lambda w: pl.BlockSpec((tm, w), lambda i: (i, 0))
    full = lambda shape: pl.BlockSpec(shape, lambda i: (0, 0))
    mem = lambda half: pl.BlockSpec((mem_len, d), lambda i: (i // tiles_per_seq, half))
    return pl.pallas_call(
        _outproj_xattn_body,
        grid=(n // tm,),
        in_specs=[row(d), row(MLSTM_W), row(ATTN_W), full((MLSTM_W, d)), full((ATTN_W, d)),
                  full((1, d)), full((d, d)), mem(0), mem(1)],
        out_specs=[row(d), row(d)],
        out_shape=[jax.ShapeDtypeStruct((n, d), F32), jax.ShapeDtypeStruct((n, d), BF16)],
        compiler_params=_cparams(("arbitrary",), 40),
        name="outproj_xattn",
    )(x, hm, ha, w_out_m, w_out_a, g_x.reshape(1, d), w_mq, zmem, zmem)


def _ffn_steps(x1_ref, o_ref, wmo_ref, gf_ref, wg_ref, wu_ref, wd_ref, gfin_ref, y_ref):
    x2 = x1_ref[...] + _dot(o_ref[...], wmo_ref[...])
    h = _rms(x2, gf_ref[...]).astype(BF16)
    yield
    g = _dot(h, wg_ref[...])
    yield
    u = _dot(h, wu_ref[...])
    yield
    a = (g * jax.nn.sigmoid(g) * u).astype(BF16)
    yield
    acc = x2 + _dot(a, wd_ref[...])
    yield
    y_ref[...] = _rms(acc, gfin_ref[...])


def _run(*gens):
    gens = list(gens)
    while gens:
        alive = []
        for g in gens:
            try:
                next(g)
                alive.append(g)
            except StopIteration:
                pass
        gens = alive


def _post_ffn_body(*refs):
    _run(_ffn_steps(*refs))


def _post_ffn(x1, o, w_mo, g_ffn, w_gate, w_up, w_down, g_final, *, tm):
    n, d = x1.shape
    dff = w_gate.shape[1]
    tm = min(tm, n)
    assert n % tm == 0
    row = lambda: pl.BlockSpec((tm, d), lambda i: (i, 0))
    full = lambda shape: pl.BlockSpec(shape, lambda i: (0, 0), pipeline_mode=pl.Buffered(1))
    weights = 2 * (d * d + 3 * d * dff)
    work = tm * (6 * d * 4 + 2 * dff * 4 + dff * 2 + d * 2)
    return pl.pallas_call(
        _post_ffn_body,
        grid=(n // tm,),
        in_specs=[row(), row(), full((d, d)), full((1, d)), full((d, dff)), full((d, dff)),
                  full((dff, d)), full((1, d))],
        out_specs=row(),
        out_shape=jax.ShapeDtypeStruct((n, d), F32),
        compiler_params=_cparams(("arbitrary",), (weights + work) // MIB + 8),
        name="post_ffn",
    )(x1, o, w_mo, g_ffn.reshape(1, d), w_gate, w_up, w_down, g_final.reshape(1, d))


def _mlstm_sample_steps(qk_ref, v_ref, om_ref, gt_ref, conv_ref, c_ref, n_ref, m_ref,
                        cw_ref, cb_ref, bg_ref, gmh_ref,
                        hm_ref, conv_out, c_out, n_out, m_out, *, steps):
    T = steps
    W2 = 2 * MLSTM_W
    b = 0
    t_idx = lax.broadcasted_iota(jnp.int32, (T, T), 0)
    s_idx = lax.broadcasted_iota(jnp.int32, (T, T), 1)
    causal = s_idx <= t_idx
    eye = s_idx == t_idx
    col = lax.broadcasted_iota(jnp.int32, (T, LANES), 1)
    pad = jnp.zeros((8 - T, MLSTM_DH), F32)

    def to_row(v_col):
        return jnp.sum(jnp.where(eye, v_col, 0.0), axis=0, keepdims=True)

    up = jnp.concatenate([conv_ref[b], qk_ref[b]], axis=0)
    y = jnp.broadcast_to(cb_ref[...], (T, W2))
    for j in range(CONV_W):
        y = y + up[j:j + T, :] * cw_ref[j:j + 1, :]
    conv_out[b] = up[T:T + CONV_W - 1, :]
    qk = y * jax.nn.sigmoid(y)
    gates = gt_ref[b] + bg_ref[...]
    gates = jnp.where(col >= MLSTM_HEADS, _log_sigmoid(gates), gates)
    v_all = v_ref[b]
    om_all = om_ref[b]
    m_all = m_ref[b]
    yield
    st = []
    for h in range(MLSTM_HEADS):
        sl = slice(h * MLSTM_DH, (h + 1) * MLSTM_DH)
        qh = qk[:, sl]
        kh = qk[:, MLSTM_W + h * MLSTM_DH:MLSTM_W + (h + 1) * MLSTM_DH] * (MLSTM_DH ** -0.5)
        vh = v_all[:, sl]
        i_col = gates[:, h:h + 1]
        f_col = gates[:, MLSTM_HEADS + h:MLSTM_HEADS + h + 1]
        c_prev = c_ref[b, h]
        q8 = jnp.concatenate([qh, pad], axis=0).astype(BF16)
        inter = _dot(q8, c_prev.astype(BF16))[0:T, :]
        st.append(dict(qh=qh, kh=kh, vh=vh, i_col=i_col, f_col=f_col, c_prev=c_prev, inter=inter,
                       i_row=to_row(i_col), f_row=to_row(f_col)))
    yield
    for h, d in enumerate(st):
        d['b_col'] = jnp.sum(jnp.where(causal, d['f_row'], 0.0), axis=1, keepdims=True)
        qkt = jnp.zeros((T, T), F32)
        for s in range(T):
            dots = jnp.sum(d['qh'] * d['kh'][s:s + 1, :], axis=1, keepdims=True)
            qkt = jnp.where(s_idx == s, dots, qkt)
        d['qkt'] = qkt
    yield
    for h, d in enumerate(st):
        m_prev = m_all[:, h:h + 1]
        n_prev = n_ref[b, h:h + 1, :]
        b_col = d['b_col']
        b_row = to_row(b_col)
        dmat = jnp.where(causal, b_col - b_row + d['i_row'], MASKED)
        a_col = b_col + m_prev
        mt = jnp.maximum(a_col, jnp.max(dmat, axis=1, keepdims=True))
        wa = jnp.exp(a_col - mt)
        sc = d['qkt'] * jnp.exp(dmat - mt)
        intra = jnp.zeros((T, MLSTM_DH), F32)
        for s in range(T):
            intra = intra + sc[:, s:s + 1] * d['vh'][s:s + 1, :]
        num = wa * d['inter'] + intra
        den = wa * jnp.sum(d['qh'] * n_prev, axis=1, keepdims=True) + jnp.sum(sc, axis=1, keepdims=True)
        d['hh'] = num / jnp.maximum(jnp.abs(den), jnp.exp(-mt))
        bl = b_col[T - 1:T, :]
        g_col = bl - b_col + d['i_col']
        m_new = jnp.maximum(bl + m_prev, jnp.max(g_col, axis=0, keepdims=True))
        d['wc'] = jnp.exp(bl + m_prev - m_new)
        kw = jnp.exp(g_col - m_new) * d['kh']
        kw8 = jnp.concatenate([kw, pad], axis=0).astype(BF16)
        v8 = jnp.concatenate([d['vh'], pad], axis=0).astype(BF16)
        d['dc'] = _dot_tn(kw8, v8)
        n_out[b, h:h + 1, :] = d['wc'] * n_prev + jnp.sum(kw, axis=0, keepdims=True)
        m_out[b, h:h + 1, :] = jnp.broadcast_to(m_new, (1, LANES))
    yield
    hs = []
    for h, d in enumerate(st):
        sl = slice(h * MLSTM_DH, (h + 1) * MLSTM_DH)
        c_out[b, h] = d['wc'] * d['c_prev'] + d['dc']
        hh = d['hh']
        mu = jnp.mean(hh, axis=1, keepdims=True)
        var = jnp.mean(jnp.square(hh - mu), axis=1, keepdims=True)
        hn = (hh - mu) * lax.rsqrt(var + EPS) * gmh_ref[:, sl]
        hs.append(hn * jax.nn.sigmoid(om_all[:, sl]))
    hm_ref[b] = jnp.concatenate(hs, axis=1)


def _branch_count(dist):
    cnt = jnp.zeros(dist.shape, F32)
    for window, d in DILATED:
        ok = (dist >= 0) & (dist <= window) & (lax.rem(jnp.maximum(dist, 0), d) == 0)
        cnt = cnt + jnp.where(ok, 1.0, 0.0)
    return cnt


def _dilated_sample_steps(q_ref, kn_ref, vn_ref, kc_ref, vc_ref, o_ref, ot_s, fresh_s, den_s, *, steps, wb):
    Q = q_ref.shape[2]
    t_c = lax.broadcasted_iota(jnp.int32, (Q, wb), 0)
    pos_c = lax.broadcasted_iota(jnp.int32, (Q, wb), 1)
    cnt_c = _branch_count(wb + t_c - pos_c)
    t_n = lax.broadcasted_iota(jnp.int32, (Q, Q), 0)
    s_n = lax.broadcasted_iota(jnp.int32, (Q, Q), 1)
    cnt_n = jnp.where(s_n < steps, _branch_count(t_n - s_n), 0.0)
    lane_t = lax.broadcasted_iota(jnp.int32, (ATTN_DH, LANES), 1)

    def head(h):
        qh = q_ref[0, h] * (ATTN_DH ** -0.5)
        knh = kn_ref[0, h]
        vnh = vn_ref[0, h]
        s_c = _dot(qh.astype(BF16), kc_ref[0, h].astype(BF16))
        s_new = jnp.zeros((Q, Q), F32)
        for s in range(steps):
            dots = jnp.sum(qh * knh[s:s + 1, :], axis=1, keepdims=True)
            s_new = jnp.where(s_n == s, dots, s_new)
        s_c = jnp.where(cnt_c > 0, s_c, MASKED)
        s_new = jnp.where(cnt_n > 0, s_new, MASKED)
        mx = jnp.maximum(jnp.max(s_c, axis=1, keepdims=True), jnp.max(s_new, axis=1, keepdims=True))
        p_c = cnt_c * jnp.exp(s_c - mx)
        p_n = cnt_n * jnp.exp(s_new - mx)
        den = jnp.sum(p_c, axis=1, keepdims=True) + jnp.sum(p_n, axis=1, keepdims=True)
        den_s[h] = jnp.broadcast_to(den, (Q, ATTN_DH))
        part = [jnp.zeros((ATTN_DH, LANES), F32) for _ in range(steps)]
        for c in range(wb // LANES):
            cs = slice(c * LANES, (c + 1) * LANES)
            v_c = vc_ref[0, h, :, cs]
            for t in range(steps):
                part[t] = part[t] + v_c * p_c[t:t + 1, cs]
        o_t = jnp.zeros((ATTN_DH, LANES), F32)
        for t in range(steps):
            o_t = jnp.where(lane_t == t, jnp.sum(part[t], axis=1, keepdims=True), o_t)
        ot_s[h * ATTN_DH:(h + 1) * ATTN_DH, :] = o_t
        o_new = jnp.zeros((Q, ATTN_DH), F32)
        for s in range(steps):
            o_new = o_new + p_n[:, s:s + 1] * vnh[s:s + 1, :]
        fresh_s[h] = o_new

    for h in range(ATTN_HEADS):
        head(h)
        if h % 2 == 1:
            yield
    o_cached = ot_s[...].T
    for h in range(ATTN_HEADS):
        o_ref[0, h] = (o_cached[0:Q, h * ATTN_DH:(h + 1) * ATTN_DH] + fresh_s[h]) / den_s[h]


N_FFN_IN, N_ATTN_IN, N_MLSTM_IN = 8, 5, 12


def _ffn_and_sample_mixers_body(*refs, steps, wb):
    ffn_in, refs = refs[:N_FFN_IN], refs[N_FFN_IN:]
    attn_in, refs = refs[:N_ATTN_IN], refs[N_ATTN_IN:]
    mlstm_in, refs = refs[:N_MLSTM_IN], refs[N_MLSTM_IN:]
    y_ref, oa_ref = refs[0], refs[1]
    mlstm_out, attn_scratch = refs[2:7], refs[7:]
    _run(_ffn_steps(*ffn_in, y_ref),
         _dilated_sample_steps(*attn_in, oa_ref, *attn_scratch, steps=steps, wb=wb),
         _mlstm_sample_steps(*mlstm_in, *mlstm_out, steps=steps))


def _ffn_and_sample_mixers(x1, o, w_mo, g_ffn, w_gate, w_up, w_down, g_final,
                           q8, kn8, vn8, kc_t, vc_t,
                           z3, gates3, conv, c0, n0, m0, conv_w, conv_b, bg_row, g_mh, *, steps):
    n, d = x1.shape
    dff = w_gate.shape[1]
    batch, _, qrows, _ = q8.shape
    wb = kc_t.shape[-1]
    W2 = 2 * MLSTM_W
    assert n % batch == 0
    tm = n // batch
    row = lambda: pl.BlockSpec((tm, d), lambda i: (i, 0))
    once = lambda shape: pl.BlockSpec(shape, lambda i: (0,) * len(shape), pipeline_mode=pl.Buffered(1))
    per_seq = lambda *shape: pl.BlockSpec((1,) + shape, lambda b: (b,) + (0,) * len(shape))
    zcols = lambda width, col0: pl.BlockSpec((1, steps, width), lambda b: (b, 0, col0 // width))
    new = lambda: per_seq(ATTN_HEADS, qrows, ATTN_DH)
    cache = lambda: per_seq(ATTN_HEADS, ATTN_DH, wb)
    weights = 2 * (d * d + 3 * d * dff)
    work = tm * (6 * d * 4 + 2 * dff * 4 + dff * 2 + d * 2)
    caches = 2 * 2 * ATTN_W * wb * 4
    state = 4 * MLSTM_HEADS * MLSTM_DH * MLSTM_DH * 4
    return pl.pallas_call(
        functools.partial(_ffn_and_sample_mixers_body, steps=steps, wb=wb),
        grid=(batch,),
        in_specs=[row(), row(), once((d, d)), once((1, d)), once((d, dff)), once((d, dff)),
                  once((dff, d)), once((1, d)),
                  new(), new(), new(), cache(), cache(),
                  zcols(W2, COL_QK), zcols(MLSTM_W, COL_VM), zcols(MLSTM_W, COL_OM),
                  per_seq(steps, LANES), per_seq(CONV_W - 1, W2),
                  per_seq(MLSTM_HEADS, MLSTM_DH, MLSTM_DH), per_seq(MLSTM_HEADS, MLSTM_DH),
                  per_seq(1, LANES),
                  once((CONV_W, W2)), once((1, W2)), once((1, LANES)), once((1, MLSTM_W))],
        out_specs=[row(), new(),
                   per_seq(steps, MLSTM_W), per_seq(CONV_W - 1, W2),
                   per_seq(MLSTM_HEADS, MLSTM_DH, MLSTM_DH), per_seq(MLSTM_HEADS, MLSTM_DH),
                   per_seq(MLSTM_HEADS, LANES)],
        out_shape=[jax.ShapeDtypeStruct((n, d), F32),
                   jax.ShapeDtypeStruct((batch, ATTN_HEADS, qrows, ATTN_DH), F32),
                   jax.ShapeDtypeStruct((batch, steps, MLSTM_W), F32),
                   jax.ShapeDtypeStruct((batch, CONV_W - 1, W2), F32),
                   jax.ShapeDtypeStruct((batch, MLSTM_HEADS, MLSTM_DH, MLSTM_DH), F32),
                   jax.ShapeDtypeStruct((batch, MLSTM_HEADS, MLSTM_DH), F32),
                   jax.ShapeDtypeStruct((batch, MLSTM_HEADS, LANES), F32)],
        scratch_shapes=[pltpu.VMEM((ATTN_W, LANES), F32),
                        pltpu.VMEM((ATTN_HEADS, qrows, ATTN_DH), F32),
                        pltpu.VMEM((ATTN_HEADS, qrows, ATTN_DH), F32)],
        compiler_params=_cparams(("arbitrary",), (weights + work + caches + state) // MIB + 10),
        name="ffn_and_sample_mixers",
    )(x1, o, w_mo, g_ffn.reshape(1, d), w_gate, w_up, w_down, g_final.reshape(1, d),
      q8, kn8, vn8, kc_t, vc_t,
      z3, z3, z3, gates3, conv, c0, n0, m0, conv_w, conv_b, bg_row, g_mh)


def _xattn_sample_body(q_ref, k_ref, v_ref, o_ref):
    Q = q_ref.shape[1]
    mem_len = k_ref.shape[1]
    rows = mem_len * MEM_HEADS
    for g in range(q_ref.shape[0]):
        k2 = k_ref[g].reshape(rows, MEM_DH).astype(BF16)
        v2 = v_ref[g].reshape(rows, MEM_DH).astype(BF16)
        q_all = q_ref[g]
        qs = jnp.concatenate([q_all[:, h * MEM_DH:(h + 1) * MEM_DH] for h in range(MEM_HEADS)], axis=0)
        s = _dot_nt(qs, k2) * (MEM_DH ** -0.5)
        q_head = lax.div(lax.broadcasted_iota(jnp.int32, s.shape, 0), Q)
        k_head = lax.rem(lax.broadcasted_iota(jnp.int32, s.shape, 1), MEM_HEADS)
        s = jnp.where(q_head == k_head, s, MASKED)
        e = jnp.exp(s - jnp.max(s, axis=1, keepdims=True))
        p = e / jnp.sum(e, axis=1, keepdims=True)
        o = _dot(p.astype(BF16), v2)
        for h in range(MEM_HEADS):
            o_ref[g, :, h * MEM_DH:(h + 1) * MEM_DH] = o[h * Q:(h + 1) * Q, :].astype(BF16)


def _xattn_sample(q8, mem_k, mem_v, *, group):
    batch, qrows, d = q8.shape
    mem_len = mem_k.shape[1]
    mem = lambda: pl.BlockSpec((group, mem_len, MEM_HEADS, MEM_DH), lambda b: (b, 0, 0, 0))
    return pl.pallas_call(
        _xattn_sample_body,
        grid=(batch // group,),
        in_specs=[pl.BlockSpec((group, qrows, d), lambda b: (b, 0, 0)), mem(), mem()],
        out_specs=pl.BlockSpec((group, qrows, d), lambda b: (b, 0, 0)),
        out_shape=jax.ShapeDtypeStruct((batch, qrows, d), BF16),
        compiler_params=_cparams(("arbitrary",), 32),
        name="xattn_sample",
    )(q8, mem_k, mem_v)


def _pad_rows(a, rows):
    return jnp.pad(a, ((0, 0), (0, rows - a.shape[1]), (0, 0)))


def kernel(x_prompt, x_sample, mem_prompt, cache_attn_k, cache_attn_v, cache_mem_k, cache_mem_v, state_conv, state_C, state_n, state_m, g_mix, w_in, conv_w, conv_b, b_gates, g_mh, w_out, g_mem, w_mk, w_mv, g_xattn, w_mq, w_mo, g_ffn, w_gate, w_up, w_down, g_final):
    B, T, D = x_prompt.shape
    SB, ST, _ = x_sample.shape
    mem_len = mem_prompt.shape[1]
    W2 = 2 * MLSTM_W
    gate0 = COL_QA
    n_gate = 2 * MLSTM_HEADS

    w_main = jnp.concatenate([w_in[:, :gate0], w_in[:, gate0 + n_gate:]], axis=1).astype(BF16)
    w_gcols = jnp.pad(w_in[:, gate0:gate0 + n_gate], ((0, 0), (0, LANES - n_gate))).astype(BF16)
    bg_row = jnp.pad(b_gates, (0, LANES - n_gate)).reshape(1, LANES)
    w_mem = jnp.concatenate([w_mk, w_mv], axis=1).astype(BF16)
    w_out_m = w_out[:MLSTM_W].astype(BF16)
    w_out_a = w_out[MLSTM_W:].astype(BF16)
    w_mq_b, w_mo_b = w_mq.astype(BF16), w_mo.astype(BF16)
    w_gate_b, w_up_b, w_down_b = w_gate.astype(BF16), w_up.astype(BF16), w_down.astype(BF16)
    conv_b2 = conv_b.reshape(1, W2)
    g_mh2 = g_mh.reshape(1, MLSTM_W)

    xp = x_prompt.reshape(B * T, D)
    zmem = _norm_matmul(mem_prompt.reshape(B * mem_len, D), g_mem, w_mem, tm=B * mem_len, tn=512)
    z, gates = _norm_matmul(xp, g_mix, w_main, w_gcols, tm=1024, tn=1792)
    hm, p_conv, p_C, p_n, p_m = _mlstm_prompt(z, gates, conv_w, conv_b2, bg_row, g_mh2,
                                              batch=B, seq=T, chunk=128, nchunk=4)
    ha = _dilated_prompt(z, batch=B, seq=T, tq=2048)
    x1, o = _outproj_xattn(xp, hm, ha, w_out_m, w_out_a, g_xattn, w_mq_b, zmem,
                           seq=T, mem_len=mem_len, tm=512)

    keep = min(DILATED[-1][0], T)
    z3p = z.reshape(B, T, N_MAIN)
    p_attn_k = z3p[:, T - keep:, COL_KA:COL_KA + ATTN_W].reshape(B, keep, ATTN_HEADS, ATTN_DH)
    p_attn_v = z3p[:, T - keep:, COL_VA:COL_VA + ATTN_W].reshape(B, keep, ATTN_HEADS, ATTN_DH)
    p_mem_k = zmem[:, :D].reshape(B, mem_len, MEM_HEADS, MEM_DH)
    p_mem_v = zmem[:, D:].reshape(B, mem_len, MEM_HEADS, MEM_DH)

    xs = x_sample.reshape(SB * ST, D)
    zs, gates_s = _norm_matmul(xs, g_mix, w_main, w_gcols, tm=SB * ST, tn=512)
    zs3 = zs.reshape(SB, ST, N_MAIN)
    m0 = jnp.pad(state_m, ((0, 0), (0, LANES - MLSTM_HEADS))).reshape(SB, 1, LANES)
    qrows = 8

    def heads_first(col0):
        a = zs3[:, :, col0:col0 + ATTN_W].reshape(SB, ST, ATTN_HEADS, ATTN_DH).transpose(0, 2, 1, 3)
        return jnp.pad(a, ((0, 0), (0, 0), (0, qrows - ST), (0, 0)))

    y_prompt, ha_s, hm_s, s_conv, s_C, s_n, s_m = _ffn_and_sample_mixers(
        x1, o, w_mo_b, g_ffn, w_gate_b, w_up_b, w_down_b, g_final,
        heads_first(COL_QA), heads_first(COL_KA), heads_first(COL_VA),
        jnp.transpose(cache_attn_k, (0, 2, 3, 1)), jnp.transpose(cache_attn_v, (0, 2, 3, 1)),
        zs3, gates_s.reshape(SB, ST, LANES), state_conv, state_C, state_n, m0,
        conv_w, conv_b2, bg_row, g_mh2, steps=ST)
    ha_s = ha_s[:, :, :ST].transpose(0, 2, 1, 3)
    x1s, qs = _outproj_q(xs, hm_s.reshape(SB * ST, MLSTM_W), ha_s.reshape(SB * ST, ATTN_W),
                         w_out_m, w_out_a, g_xattn, w_mq_b, tm=SB * ST)
    os_ = _xattn_sample(_pad_rows(qs.reshape(SB, ST, D), qrows), cache_mem_k, cache_mem_v, group=4)[:, :ST]
    y_sample = _post_ffn(x1s, os_.reshape(SB * ST, D), w_mo_b, g_ffn, w_gate_b, w_up_b, w_down_b,
                         g_final, tm=512)

    s_attn_k = zs3[:, :, COL_KA:COL_KA + ATTN_W].reshape(SB, ST, ATTN_HEADS, ATTN_DH)
    s_attn_v = zs3[:, :, COL_VA:COL_VA + ATTN_W].reshape(SB, ST, ATTN_HEADS, ATTN_DH)

    return (y_prompt.reshape(B, T, D), y_sample.reshape(SB, ST, D), p_attn_k, p_attn_v,
            p_conv, p_C, p_n, p_m[:, :, 0], p_mem_k, p_mem_v,
            s_attn_k, s_attn_v, s_conv, s_C, s_n, s_m[:, :, 0])
```

```python
import functools

import jax
import jax.numpy as jnp
from jax import lax
from jax.experimental import pallas as pl
from jax.experimental.pallas import tpu as pltpu

F32 = jnp.float32
BF16 = jnp.bfloat16

EPS = 1e-6
MASKED = -1e30

MLSTM_HEADS = 4
MLSTM_DH = 128
MLSTM_W = MLSTM_HEADS * MLSTM_DH
ATTN_HEADS = 8
ATTN_DH = 64
ATTN_W = ATTN_HEADS * ATTN_DH
DILATED = ((128, 1), (512, 4), (2048, 16))
REL = 128
CONV_W = 4
MEM_HEADS = 4
MEM_DH = 256
LANES = 128
MIB = 1024 * 1024
PAD_PITCH = 24

COL_QK = 0
COL_VM = 2 * MLSTM_W
COL_OM = 3 * MLSTM_W
COL_QA = 4 * MLSTM_W
COL_KA = COL_QA + ATTN_W
COL_VA = COL_KA + ATTN_W
N_MAIN = COL_VA + ATTN_W


def _cparams(semantics, vmem_mib):
    return pltpu.CompilerParams(dimension_semantics=semantics, vmem_limit_bytes=vmem_mib * MIB)


def _rms(x, g):
    return x * lax.rsqrt(jnp.mean(x * x, axis=-1, keepdims=True) + EPS) * g


def _log_sigmoid(x):
    return jnp.minimum(x, 0.0) - jnp.log1p(jnp.exp(-jnp.abs(x)))


def _dot(a, b):
    return jnp.dot(a, b, preferred_element_type=F32)


def _dot_nt(a, b):
    return lax.dot_general(a, b, (((1,), (1,)), ((), ())), preferred_element_type=F32)


def _dot_tn(a, b):
    return lax.dot_general(a, b, (((0,), (0,)), ((), ())), preferred_element_type=F32)


def _norm_matmul_body(x_ref, g_ref, w_ref, z_ref, h_ref):
    @pl.when(pl.program_id(1) == 0)
    def _():
        h_ref[...] = _rms(x_ref[...], g_ref[...]).astype(BF16)

    z_ref[...] = _dot(h_ref[...], w_ref[...])


def _norm_matmul(x, g, w, *, tm, tn):
    n, d = x.shape
    c = w.shape[1]
    vmem = 2 * tm * d * 4 + tm * d * 2 + 2 * d * tn * 2 + 2 * tm * tn * 4
    return pl.pallas_call(
        _norm_matmul_body,
        grid=(n // tm, c // tn),
        in_specs=[pl.BlockSpec((tm, d), lambda i, j: (i, 0)), pl.BlockSpec((1, d), lambda i, j: (0, 0)),
                  pl.BlockSpec((d, tn), lambda i, j: (0, j))],
        out_specs=pl.BlockSpec((tm, tn), lambda i, j: (i, j)),
        out_shape=jax.ShapeDtypeStruct((n, c), F32),
        scratch_shapes=[pltpu.VMEM((tm, d), BF16)],
        compiler_params=_cparams(("arbitrary", "arbitrary"), vmem // MIB + 8),
        name="norm_matmul",
    )(x, g.reshape(1, d), w)


def _inproj_body(x_ref, g_ref, wt_ref, wst_ref, z_ref, side_ref):
    h = _rms(x_ref[...], g_ref[...]).astype(BF16)
    side_ref[...] = _dot_nt(h, wst_ref[...])
    z_ref[...] = _dot_nt(h, wt_ref[...])


def _inproj(x, g, w_t, w_side_t, *, tm):
    n, d = x.shape
    c = w_t.shape[0]
    tm = min(tm, n)
    assert n % tm == 0
    once = lambda shape: pl.BlockSpec(shape, lambda i: (0, 0), pipeline_mode=pl.Buffered(1))
    vmem = 2 * tm * d * 4 + d * (c + LANES) * 2 + 2 * tm * (c + LANES) * 4 + tm * c * 4
    return pl.pallas_call(
        _inproj_body,
        grid=(n // tm,),
        in_specs=[pl.BlockSpec((tm, d), lambda i: (i, 0)), once((1, d)), once((c, d)), once((LANES, d))],
        out_specs=[pl.BlockSpec((tm, c), lambda i: (i, 0)), pl.BlockSpec((tm, LANES), lambda i: (i, 0))],
        out_shape=[jax.ShapeDtypeStruct((n, c), F32), jax.ShapeDtypeStruct((n, LANES), F32)],
        compiler_params=_cparams(("arbitrary",), vmem // MIB + 8),
        name="inproj",
    )(x, g.reshape(1, d), w_t, w_side_t)


def _mlstm_prompt_body(qk_ref, v_ref, om_ref, gt_ref, cw_ref, cb_ref, bg_ref, gmh_ref,
                       hm_ref, conv_ref, c_out, n_out, m_out,
                       ubuf, c_s, n_s, m_s, *, chunk, nchunk):
    L = chunk
    H = MLSTM_HEADS
    rows_step = L * nchunk
    c = pl.program_id(1)
    last = pl.num_programs(1) - 1
    W2 = 2 * MLSTM_W

    @pl.when(c == 0)
    def _():
        ubuf[0:8, :] = jnp.zeros((8, W2), F32)
        c_s[...] = jnp.zeros_like(c_s)
        n_s[...] = jnp.zeros_like(n_s)
        m_s[...] = jnp.zeros_like(m_s)

    @pl.when(c > 0)
    def _():
        ubuf[0:8, :] = ubuf[rows_step:rows_step + 8, :]

    ubuf[8:rows_step + 8, :] = qk_ref[...]

    t_idx = lax.broadcasted_iota(jnp.int32, (L, L), 0)
    s_idx = lax.broadcasted_iota(jnp.int32, (L, L), 1)
    causal = s_idx <= t_idx
    tril = jnp.where(causal, 1.0, 0.0).astype(BF16)
    col = lax.broadcasted_iota(jnp.int32, (L, LANES), 1)

    units = [(ci, h) for ci in range(nchunk) for h in range(H)]
    pre = []
    for ci in range(nchunk):
        r0 = ci * L
        y = jnp.broadcast_to(cb_ref[...], (L, W2))
        for j in range(CONV_W):
            lo = r0 + 8 - (CONV_W - 1) + j
            y = y + ubuf[lo:lo + L, :] * cw_ref[j:j + 1, :]
        qk = y * jax.nn.sigmoid(y)
        gates = gt_ref[r0:r0 + L, :] + bg_ref[...]
        gates = jnp.where(col >= H, _log_sigmoid(gates), gates)
        g_hi = gates.astype(BF16)
        g_lo = (gates - g_hi.astype(F32)).astype(BF16)
        csum = _dot(tril, g_hi) + _dot(tril, g_lo)
        pre.append((qk, gates, csum, gates.T, csum.T))

    def unit(ci, h):
        qk, gates, csum, gates_t, csum_t = pre[ci]
        sl = slice(h * MLSTM_DH, (h + 1) * MLSTM_DH)
        qh = qk[:, sl]
        kh = qk[:, MLSTM_W + h * MLSTM_DH:MLSTM_W + (h + 1) * MLSTM_DH] * (MLSTM_DH ** -0.5)
        b_row = csum_t[H + h:H + h + 1, :]
        return dict(qh=qh, kh=kh, qb=qh.astype(BF16), kb=kh.astype(BF16),
                    vb=v_ref[ci * L:(ci + 1) * L, sl].astype(BF16),
                    i_col=gates[:, h:h + 1], i_row=gates_t[h:h + 1, :],
                    b_col=csum[:, H + h:H + h + 1], b_row=b_row, bl=b_row[:, L - 1:L])

    U = {u: unit(*u) for u in units}
    for d in U.values():
        d['dmat'] = jnp.where(causal, d['b_col'] - d['b_row'] + d['i_row'], MASKED)
        d['qk'] = _dot_nt(d['qb'], d['kb'])
        d['g_col'] = d['bl'] - d['b_col'] + d['i_col']
    for d in U.values():
        d['m_loc'] = jnp.max(d['dmat'], axis=1, keepdims=True)
        d['g_max'] = jnp.max(d['g_col'], axis=0, keepdims=True)
    for d in U.values():
        d['sc'] = d['qk'] * jnp.exp(d['dmat'] - d['m_loc'])
        d['kw'] = jnp.exp(d['g_col'] - d['g_max']) * d['kh']
    for d in U.values():
        d['a_loc'] = _dot(d['sc'].astype(BF16), d['vb'])
        d['r_loc'] = jnp.sum(d['sc'], axis=1, keepdims=True)
        d['dc_loc'] = _dot_tn(d['kw'].astype(BF16), d['vb'])
        d['dn_loc'] = jnp.sum(d['kw'], axis=0, keepdims=True)

    c_st = [c_s[h] for h in range(H)]
    n_st = [n_s[h:h + 1, :] for h in range(H)]
    m_st = [m_s[h:h + 1, 0:1] for h in range(H)]
    for (ci, h), d in U.items():
        m_prev, c_prev, n_prev = m_st[h], c_st[h], n_st[h]
        a_col = d['b_col'] + m_prev
        mt = jnp.maximum(a_col, d['m_loc'])
        wa = jnp.exp(a_col - mt)
        wl = jnp.exp(d['m_loc'] - mt)
        num = wa * _dot(d['qb'], c_prev.astype(BF16)) + wl * d['a_loc']
        den = wa * jnp.sum(d['qh'] * n_prev, axis=1, keepdims=True) + wl * d['r_loc']
        d['hh'] = num / jnp.maximum(jnp.abs(den), jnp.exp(-mt))
        m_new = jnp.maximum(d['bl'] + m_prev, d['g_max'])
        wc = jnp.exp(d['bl'] + m_prev - m_new)
        wg = jnp.exp(d['g_max'] - m_new)
        c_st[h] = wc * c_prev + wg * d['dc_loc']
        n_st[h] = wc * n_prev + wg * d['dn_loc']
        m_st[h] = m_new
    for h in range(H):
        c_s[h] = c_st[h]
        n_s[h:h + 1, :] = n_st[h]
        m_s[h:h + 1, :] = jnp.broadcast_to(m_st[h], (1, LANES))

    for (ci, h), d in U.items():
        sl = slice(h * MLSTM_DH, (h + 1) * MLSTM_DH)
        hh = d['hh']
        mu = jnp.mean(hh, axis=1, keepdims=True)
        var = jnp.mean(jnp.square(hh - mu), axis=1, keepdims=True)
        hn = (hh - mu) * lax.rsqrt(var + EPS) * gmh_ref[:, sl]
        hm_ref[ci * L:(ci + 1) * L, sl] = hn * jax.nn.sigmoid(om_ref[ci * L:(ci + 1) * L, sl])

    @pl.when(c == last)
    def _():
        conv_ref[0] = ubuf[rows_step + 8 - (CONV_W - 1):rows_step + 8, :]
        c_out[0] = c_s[...]
        n_out[0] = n_s[...]
        m_out[0] = m_s[...]


def _mlstm_prompt(z, gates, conv_w, conv_b, bg_row, g_mh, *, batch, seq, chunk, nchunk):
    rows = chunk * nchunk
    nc = seq // rows
    n = batch * seq
    row = lambda b, c: b * nc + c
    W2 = 2 * MLSTM_W
    full = lambda shape: pl.BlockSpec(shape, lambda b, c: (0,) * len(shape))
    return pl.pallas_call(
        functools.partial(_mlstm_prompt_body, chunk=chunk, nchunk=nchunk),
        grid=(batch, nc),
        in_specs=[
            pl.BlockSpec((rows, W2), lambda b, c: (row(b, c), COL_QK // W2)),
            pl.BlockSpec((rows, MLSTM_W), lambda b, c: (row(b, c), COL_VM // MLSTM_W)),
            pl.BlockSpec((rows, MLSTM_W), lambda b, c: (row(b, c), COL_OM // MLSTM_W)),
            pl.BlockSpec((rows, LANES), lambda b, c: (row(b, c), 0)),
            full((CONV_W, W2)), full((1, W2)), full((1, LANES)), full((1, MLSTM_W)),
        ],
        out_specs=[
            pl.BlockSpec((rows, MLSTM_W), lambda b, c: (row(b, c), 0)),
            pl.BlockSpec((1, CONV_W - 1, W2), lambda b, c: (b, 0, 0)),
            pl.BlockSpec((1, MLSTM_HEADS, MLSTM_DH, MLSTM_DH), lambda b, c: (b, 0, 0, 0)),
            pl.BlockSpec((1, MLSTM_HEADS, MLSTM_DH), lambda b, c: (b, 0, 0)),
            pl.BlockSpec((1, MLSTM_HEADS, LANES), lambda b, c: (b, 0, 0)),
        ],
        out_shape=[
            jax.ShapeDtypeStruct((n, MLSTM_W), F32),
            jax.ShapeDtypeStruct((batch, CONV_W - 1, W2), F32),
            jax.ShapeDtypeStruct((batch, MLSTM_HEADS, MLSTM_DH, MLSTM_DH), F32),
            jax.ShapeDtypeStruct((batch, MLSTM_HEADS, MLSTM_DH), F32),
            jax.ShapeDtypeStruct((batch, MLSTM_HEADS, LANES), F32),
        ],
        scratch_shapes=[
            pltpu.VMEM((rows + 8, W2), F32),
            pltpu.VMEM((MLSTM_HEADS, MLSTM_DH, MLSTM_DH), F32),
            pltpu.VMEM((MLSTM_HEADS, MLSTM_DH), F32),
            pltpu.VMEM((MLSTM_HEADS, LANES), F32),
        ],
        compiler_params=_cparams(("arbitrary", "arbitrary"), 32),
        name="mlstm_prompt",
    )(z, z, z, gates, conv_w, conv_b, bg_row, g_mh)


def _dilated_prompt_body(q_ref, k_ref, kp_ref, v_ref, vp_ref, o_ref, kpad, vpad, acc, mm, ll, bias, *, tq):
    i = pl.program_id(2)
    hd = ATTN_DH
    dmax = max(d for _, d in DILATED)
    ngrp = tq // dmax
    half = ngrp * PAD_PITCH
    cur = lax.rem(i, 2) * half
    prev = lax.rem(i + 1, 2) * half

    @pl.when(i == 0)
    def _():
        kpad[pl.ds(pl.multiple_of(prev, 8), half), :] = jnp.zeros((half, LANES), F32)
        vpad[pl.ds(pl.multiple_of(prev, 8), half), :] = jnp.zeros((half, LANES), F32)

    def pad_copy(g, carry):
        src = pl.ds(pl.multiple_of(g * dmax, dmax), dmax)
        dst = pl.ds(pl.multiple_of(cur + g * PAD_PITCH, 8), dmax)
        kpad[dst, :] = k_ref[src, :]
        vpad[dst, :] = v_ref[src, :]
        return carry

    lax.fori_loop(0, ngrp, pad_copy, 0, unroll=8)

    lane = lax.broadcasted_iota(jnp.int32, (REL, LANES), 1)
    first_head = lane < hd
    a_idx = lax.broadcasted_iota(jnp.int32, (REL, 2 * REL), 0)
    j_idx = lax.broadcasted_iota(jnp.int32, (REL, 2 * REL), 1)
    band = (j_idx >= a_idx) & (j_idx <= a_idx + REL)
    b0 = jnp.where(band, 0.0, MASKED)
    b1 = jnp.where(band & (j_idx >= jnp.where(i == 0, REL, 0)), 0.0, MASKED)
    bias[0] = jnp.concatenate([b0, b0], axis=0)
    bias[1] = jnp.concatenate([b1, b1], axis=0)

    order = sorted(DILATED, key=lambda wd: -wd[1])
    for bi, (window, d) in enumerate(order):
        assert window // d == REL and tq % (REL * d) == 0
        nblk = tq // (REL * d)
        for r in range(d):
            for cb in range(nblk):
                rows = pl.ds(r + d * REL * cb, REL, stride=d)
                q2 = q_ref[rows, :] * (hd ** -0.5)
                if d == dmax:
                    kp = kpad[pl.ds(prev + r, REL, stride=PAD_PITCH), :]
                    kc = kpad[pl.ds(cur + r, REL, stride=PAD_PITCH), :]
                    vp = vpad[pl.ds(prev + r, REL, stride=PAD_PITCH), :]
                    vc = vpad[pl.ds(cur + r, REL, stride=PAD_PITCH), :]
                    k2 = jnp.concatenate([kp, kc], axis=0)
                    v2 = jnp.concatenate([vp, vc], axis=0)
                elif cb == 0:
                    tail = pl.ds(r + d * REL * (nblk - 1), REL, stride=d)
                    k2 = jnp.concatenate([kp_ref[tail, :], k_ref[rows, :]], axis=0)
                    v2 = jnp.concatenate([vp_ref[tail, :], v_ref[rows, :]], axis=0)
                else:
                    both = pl.ds(r + d * REL * (cb - 1), 2 * REL, stride=d)
                    k2 = k_ref[both, :]
                    v2 = v_ref[both, :]
                qs = jnp.concatenate([jnp.where(first_head, q2, 0.0), jnp.where(first_head, 0.0, q2)], axis=0)
                s = _dot_nt(qs.astype(BF16), k2.astype(BF16)) + bias[1 if cb == 0 else 0]
                mx = jnp.max(s, axis=1, keepdims=True)
                p = jnp.exp(s - mx)
                l = jnp.sum(p, axis=1, keepdims=True)
                o = _dot(p.astype(BF16), v2.astype(BF16))
                m_blk = jnp.where(first_head, mx[0:REL], mx[REL:2 * REL])
                l_blk = jnp.where(first_head, l[0:REL], l[REL:2 * REL])
                o_blk = jnp.where(first_head, o[0:REL], o[REL:2 * REL])
                if bi == 0:
                    mm[rows, :] = m_blk
                    ll[rows, :] = l_blk
                    acc[rows, :] = o_blk
                else:
                    m_old = mm[rows, :]
                    m_new = jnp.maximum(m_old, m_blk)
                    w_old = jnp.exp(m_old - m_new)
                    w_blk = jnp.exp(m_blk - m_new)
                    mm[rows, :] = m_new
                    ll[rows, :] = ll[rows, :] * w_old + l_blk * w_blk
                    acc[rows, :] = acc[rows, :] * w_old + o_blk * w_blk

    o_ref[...] = acc[...] / ll[...]


def _dilated_prompt(z, *, batch, seq, tq):
    nt = seq // tq
    n = batch * seq
    pairs = ATTN_W // LANES
    dmax = max(d for _, d in DILATED)
    assert tq == REL * dmax
    pad_rows = 2 * (tq // dmax) * PAD_PITCH

    def spec(col0, back):
        return pl.BlockSpec((tq, LANES),
                            lambda b, p, i: (b * nt + jnp.maximum(i - back, 0), col0 // LANES + p))

    return pl.pallas_call(
        functools.partial(_dilated_prompt_body, tq=tq),
        grid=(batch, pairs, nt),
        in_specs=[spec(COL_QA, 0), spec(COL_KA, 0), spec(COL_KA, 1), spec(COL_VA, 0), spec(COL_VA, 1)],
        out_specs=pl.BlockSpec((tq, LANES), lambda b, p, i: (b * nt + i, p)),
        out_shape=jax.ShapeDtypeStruct((n, ATTN_W), F32),
        scratch_shapes=[
            pltpu.VMEM((pad_rows, LANES), F32), pltpu.VMEM((pad_rows, LANES), F32),
            pltpu.VMEM((tq, LANES), F32), pltpu.VMEM((tq, LANES), F32), pltpu.VMEM((tq, LANES), F32),
            pltpu.VMEM((2, 2 * REL, 2 * REL), F32),
        ],
        compiler_params=_cparams(("arbitrary", "arbitrary", "arbitrary"), 32),
        name="dilated_prompt",
    )(z, z, z, z, z)


def _outproj_q_body(x_ref, hm_ref, ha_ref, wom_ref, woa_ref, gx_ref, wq_ref, x1_ref, q_ref):
    x1 = x_ref[...] + _dot(hm_ref[...].astype(BF16), wom_ref[...]) + _dot(ha_ref[...].astype(BF16), woa_ref[...])
    x1_ref[...] = x1
    q_ref[...] = _dot(_rms(x1, gx_ref[...]).astype(BF16), wq_ref[...]).astype(BF16)


def _outproj_q(x, hm, ha, w_out_m, w_out_a, g_x, w_mq, *, tm):
    n, d = x.shape
    row = lambda w: pl.BlockSpec((tm, w), lambda i: (i, 0))
    full = lambda shape: pl.BlockSpec(shape, lambda i: (0, 0))
    return pl.pallas_call(
        _outproj_q_body,
        grid=(n // tm,),
        in_specs=[row(d), row(MLSTM_W), row(ATTN_W), full((MLSTM_W, d)), full((ATTN_W, d)),
                  full((1, d)), full((d, d))],
        out_specs=[row(d), row(d)],
        out_shape=[jax.ShapeDtypeStruct((n, d), F32), jax.ShapeDtypeStruct((n, d), BF16)],
        compiler_params=_cparams(("arbitrary",), 40),
        name="outproj_q",
    )(x, hm, ha, w_out_m, w_out_a, g_x.reshape(1, d), w_mq)


def _outproj_xattn_body(x_ref, hm_ref, ha_ref, wom_ref, woa_ref, gx_ref, wq_ref, k_ref, v_ref,
                        x1_ref, o_ref):
    x1 = x_ref[...] + _dot(hm_ref[...].astype(BF16), wom_ref[...]) + _dot(ha_ref[...].astype(BF16), woa_ref[...])
    x1_ref[...] = x1
    q = _dot(_rms(x1, gx_ref[...]).astype(BF16), wq_ref[...]).astype(BF16)
    for h in range(MEM_HEADS):
        sl = slice(h * MEM_DH, (h + 1) * MEM_DH)
        s = _dot_nt(q[:, sl], k_ref[:, sl].astype(BF16)) * (MEM_DH ** -0.5)
        e = jnp.exp(s - jnp.max(s, axis=1, keepdims=True))
        p = e / jnp.sum(e, axis=1, keepdims=True)
        o_ref[:, sl] = _dot(p.astype(BF16), v_ref[:, sl].astype(BF16)).astype(BF16)


def _outproj_xattn(x, hm, ha, w_out_m, w_out_a, g_x, w_mq, zmem, *, seq, mem_len, tm):
    n, d = x.shape
    assert seq % tm == 0
    tiles_per_seq = seq // tm
    row = lambda w: pl.BlockSpec((tm, w), lambda i: (i, 0))
    full = lambda shape: pl.BlockSpec(shape, lambda i: (0, 0))
    mem = lambda half: pl.BlockSpec((mem_len, d), lambda i: (i // tiles_per_seq, half))
    return pl.pallas_call(
        _outproj_xattn_body,
        grid=(n // tm,),
        in_specs=[row(d), row(MLSTM_W), row(ATTN_W), full((MLSTM_W, d)), full((ATTN_W, d)),
                  full((1, d)), full((d, d)), mem(0), mem(1)],
        out_specs=[row(d), row(d)],
        out_shape=[jax.ShapeDtypeStruct((n, d), F32), jax.ShapeDtypeStruct((n, d), BF16)],
        compiler_params=_cparams(("arbitrary",), 40),
        name="outproj_xattn",
    )(x, hm, ha, w_out_m, w_out_a, g_x.reshape(1, d), w_mq, zmem, zmem)


def _ffn_steps(x1_ref, o_ref, wmo_ref, gf_ref, wg_ref, wu_ref, wd_ref, gfin_ref, y_ref):
    x2 = x1_ref[...] + _dot(o_ref[...], wmo_ref[...])
    h = _rms(x2, gf_ref[...]).astype(BF16)
    yield
    g = _dot(h, wg_ref[...])
    yield
    u = _dot(h, wu_ref[...])
    yield
    a = (g * jax.nn.sigmoid(g) * u).astype(BF16)
    yield
    acc = x2 + _dot(a, wd_ref[...])
    yield
    y_ref[...] = _rms(acc, gfin_ref[...])


def _run(*gens):
    gens = list(gens)
    while gens:
        alive = []
        for g in gens:
            try:
                next(g)
                alive.append(g)
            except StopIteration:
                pass
        gens = alive


def _post_ffn_body(*refs):
    _run(_ffn_steps(*refs))


def _post_ffn(x1, o, w_mo, g_ffn, w_gate, w_up, w_down, g_final, *, tm):
    n, d = x1.shape
    dff = w_gate.shape[1]
    tm = min(tm, n)
    assert n % tm == 0
    row = lambda: pl.BlockSpec((tm, d), lambda i: (i, 0))
    full = lambda shape: pl.BlockSpec(shape, lambda i: (0, 0), pipeline_mode=pl.Buffered(1))
    weights = 2 * (d * d + 3 * d * dff)
    work = tm * (6 * d * 4 + 2 * dff * 4 + dff * 2 + d * 2)
    return pl.pallas_call(
        _post_ffn_body,
        grid=(n // tm,),
        in_specs=[row(), row(), full((d, d)), full((1, d)), full((d, dff)), full((d, dff)),
                  full((dff, d)), full((1, d))],
        out_specs=row(),
        out_shape=jax.ShapeDtypeStruct((n, d), F32),
        compiler_params=_cparams(("arbitrary",), (weights + work) // MIB + 8),
        name="post_ffn",
    )(x1, o, w_mo, g_ffn.reshape(1, d), w_gate, w_up, w_down, g_final.reshape(1, d))


def _mlstm_sample_steps(qk_ref, v_ref, om_ref, gt_ref, conv_ref, c_ref, n_ref, m_ref,
                        cw_ref, cb_ref, bg_ref, gmh_ref,
                        hm_ref, conv_out, c_out, n_out, m_out, *, steps):
    T = steps
    W2 = 2 * MLSTM_W
    b = 0
    t_idx = lax.broadcasted_iota(jnp.int32, (T, T), 0)
    s_idx = lax.broadcasted_iota(jnp.int32, (T, T), 1)
    causal = s_idx <= t_idx
    eye = s_idx == t_idx
    col = lax.broadcasted_iota(jnp.int32, (T, LANES), 1)
    pad = jnp.zeros((8 - T, MLSTM_DH), F32)

    def to_row(v_col):
        return jnp.sum(jnp.where(eye, v_col, 0.0), axis=0, keepdims=True)

    up = jnp.concatenate([conv_ref[b], qk_ref[b]], axis=0)
    y = jnp.broadcast_to(cb_ref[...], (T, W2))
    for j in range(CONV_W):
        y = y + up[j:j + T, :] * cw_ref[j:j + 1, :]
    conv_out[b] = up[T:T + CONV_W - 1, :]
    qk = y * jax.nn.sigmoid(y)
    gates = gt_ref[b] + bg_ref[...]
    gates = jnp.where(col >= MLSTM_HEADS, _log_sigmoid(gates), gates)
    v_all = v_ref[b]
    om_all = om_ref[b]
    m_all = m_ref[b]
    yield
    st = []
    for h in range(MLSTM_HEADS):
        sl = slice(h * MLSTM_DH, (h + 1) * MLSTM_DH)
        qh = qk[:, sl]
        kh = qk[:, MLSTM_W + h * MLSTM_DH:MLSTM_W + (h + 1) * MLSTM_DH] * (MLSTM_DH ** -0.5)
        vh = v_all[:, sl]
        i_col = gates[:, h:h + 1]
        f_col = gates[:, MLSTM_HEADS + h:MLSTM_HEADS + h + 1]
        c_prev = c_ref[b, h]
        q8 = jnp.concatenate([qh, pad], axis=0).astype(BF16)
        inter = _dot(q8, c_prev.astype(BF16))[0:T, :]
        st.append(dict(qh=qh, kh=kh, vh=vh, i_col=i_col, f_col=f_col, c_prev=c_prev, inter=inter,
                       i_row=to_row(i_col), f_row=to_row(f_col)))
    yield
    for h, d in enumerate(st):
        d['b_col'] = jnp.sum(jnp.where(causal, d['f_row'], 0.0), axis=1, keepdims=True)
        qkt = jnp.zeros((T, T), F32)
        for s in range(T):
            dots = jnp.sum(d['qh'] * d['kh'][s:s + 1, :], axis=1, keepdims=True)
            qkt = jnp.where(s_idx == s, dots, qkt)
        d['qkt'] = qkt
    yield
    for h, d in enumerate(st):
        m_prev = m_all[:, h:h + 1]
        n_prev = n_ref[b, h:h + 1, :]
        b_col = d['b_col']
        b_row = to_row(b_col)
        dmat = jnp.where(causal, b_col - b_row + d['i_row'], MASKED)
        a_col = b_col + m_prev
        mt = jnp.maximum(a_col, jnp.max(dmat, axis=1, keepdims=True))
        wa = jnp.exp(a_col - mt)
        sc = d['qkt'] * jnp.exp(dmat - mt)
        intra = jnp.zeros((T, MLSTM_DH), F32)
        for s in range(T):
            intra = intra + sc[:, s:s + 1] * d['vh'][s:s + 1, :]
        num = wa * d['inter'] + intra
        den = wa * jnp.sum(d['qh'] * n_prev, axis=1, keepdims=True) + jnp.sum(sc, axis=1, keepdims=True)
        d['hh'] = num / jnp.maximum(jnp.abs(den), jnp.exp(-mt))
        bl = b_col[T - 1:T, :]
        g_col = bl - b_col + d['i_col']
        m_new = jnp.maximum(bl + m_prev, jnp.max(g_col, axis=0, keepdims=True))
        d['wc'] = jnp.exp(bl + m_prev - m_new)
        kw = jnp.exp(g_col - m_new) * d['kh']
        kw8 = jnp.concatenate([kw, pad], axis=0).astype(BF16)
        v8 = jnp.concatenate([d['vh'], pad], axis=0).astype(BF16)
        d['dc'] = _dot_tn(kw8, v8)
        n_out[b, h:h + 1, :] = d['wc'] * n_prev + jnp.sum(kw, axis=0, keepdims=True)
        m_out[b, h:h + 1, :] = jnp.broadcast_to(m_new, (1, LANES))
    yield
    hs = []
    for h, d in enumerate(st):
        sl = slice(h * MLSTM_DH, (h + 1) * MLSTM_DH)
        c_out[b, h] = d['wc'] * d['c_prev'] + d['dc']
        hh = d['hh']
        mu = jnp.mean(hh, axis=1, keepdims=True)
        var = jnp.mean(jnp.square(hh - mu), axis=1, keepdims=True)
        hn = (hh - mu) * lax.rsqrt(var + EPS) * gmh_ref[:, sl]
        hs.append(hn * jax.nn.sigmoid(om_all[:, sl]))
    hm_ref[b] = jnp.concatenate(hs, axis=1)


def _branch_count(dist):
    cnt = jnp.zeros(dist.shape, F32)
    for window, d in DILATED:
        ok = (dist >= 0) & (dist <= window) & (lax.rem(jnp.maximum(dist, 0), d) == 0)
        cnt = cnt + jnp.where(ok, 1.0, 0.0)
    return cnt


def _dilated_sample_steps(q_ref, kn_ref, vn_ref, kc_ref, vc_ref, o_ref, ot_s, fresh_s, den_s, *, steps, wb):
    Q = q_ref.shape[2]
    t_c = lax.broadcasted_iota(jnp.int32, (Q, wb), 0)
    pos_c = lax.broadcasted_iota(jnp.int32, (Q, wb), 1)
    cnt_c = _branch_count(wb + t_c - pos_c)
    t_n = lax.broadcasted_iota(jnp.int32, (Q, Q), 0)
    s_n = lax.broadcasted_iota(jnp.int32, (Q, Q), 1)
    cnt_n = jnp.where(s_n < steps, _branch_count(t_n - s_n), 0.0)
    lane_t = lax.broadcasted_iota(jnp.int32, (ATTN_DH, LANES), 1)

    def head(h):
        qh = q_ref[0, h] * (ATTN_DH ** -0.5)
        knh = kn_ref[0, h]
        vnh = vn_ref[0, h]
        s_c = _dot(qh.astype(BF16), kc_ref[0, h].astype(BF16))
        s_new = jnp.zeros((Q, Q), F32)
        for s in range(steps):
            dots = jnp.sum(qh * knh[s:s + 1, :], axis=1, keepdims=True)
            s_new = jnp.where(s_n == s, dots, s_new)
        s_c = jnp.where(cnt_c > 0, s_c, MASKED)
        s_new = jnp.where(cnt_n > 0, s_new, MASKED)
        mx = jnp.maximum(jnp.max(s_c, axis=1, keepdims=True), jnp.max(s_new, axis=1, keepdims=True))
        p_c = cnt_c * jnp.exp(s_c - mx)
        p_n = cnt_n * jnp.exp(s_new - mx)
        den = jnp.sum(p_c, axis=1, keepdims=True) + jnp.sum(p_n, axis=1, keepdims=True)
        den_s[h] = jnp.broadcast_to(den, (Q, ATTN_DH))
        part = [jnp.zeros((ATTN_DH, LANES), F32) for _ in range(steps)]
        for c in range(wb // LANES):
            cs = slice(c * LANES, (c + 1) * LANES)
            v_c = vc_ref[0, h, :, cs]
            for t in range(steps):
                part[t] = part[t] + v_c * p_c[t:t + 1, cs]
        o_t = jnp.zeros((ATTN_DH, LANES), F32)
        for t in range(steps):
            o_t = jnp.where(lane_t == t, jnp.sum(part[t], axis=1, keepdims=True), o_t)
        ot_s[h * ATTN_DH:(h + 1) * ATTN_DH, :] = o_t
        o_new = jnp.zeros((Q, ATTN_DH), F32)
        for s in range(steps):
            o_new = o_new + p_n[:, s:s + 1] * vnh[s:s + 1, :]
        fresh_s[h] = o_new

    for h in range(ATTN_HEADS):
        head(h)
        if h % 2 == 1:
            yield
    o_cached = ot_s[...].T
    for h in range(ATTN_HEADS):
        o_ref[0, h] = (o_cached[0:Q, h * ATTN_DH:(h + 1) * ATTN_DH] + fresh_s[h]) / den_s[h]


N_FFN_IN, N_ATTN_IN, N_MLSTM_IN = 8, 5, 12


def _ffn_and_sample_mixers_body(*refs, steps, wb):
    ffn_in, refs = refs[:N_FFN_IN], refs[N_FFN_IN:]
    attn_in, refs = refs[:N_ATTN_IN], refs[N_ATTN_IN:]
    mlstm_in, refs = refs[:N_MLSTM_IN], refs[N_MLSTM_IN:]
    y_ref, oa_ref = refs[0], refs[1]
    mlstm_out, attn_scratch = refs[2:7], refs[7:]
    _run(_ffn_steps(*ffn_in, y_ref),
         _dilated_sample_steps(*attn_in, oa_ref, *attn_scratch, steps=steps, wb=wb),
         _mlstm_sample_steps(*mlstm_in, *mlstm_out, steps=steps))


def _ffn_and_sample_mixers(x1, o, w_mo, g_ffn, w_gate, w_up, w_down, g_final,
                           q8, kn8, vn8, kc_t, vc_t,
                           z3, gates3, conv, c0, n0, m0, conv_w, conv_b, bg_row, g_mh, *, steps):
    n, d = x1.shape
    dff = w_gate.shape[1]
    batch, _, qrows, _ = q8.shape
    wb = kc_t.shape[-1]
    W2 = 2 * MLSTM_W
    assert n % batch == 0
    tm = n // batch
    row = lambda: pl.BlockSpec((tm, d), lambda i: (i, 0))
    once = lambda shape: pl.BlockSpec(shape, lambda i: (0,) * len(shape), pipeline_mode=pl.Buffered(1))
    per_seq = lambda *shape: pl.BlockSpec((1,) + shape, lambda b: (b,) + (0,) * len(shape))
    zcols = lambda width, col0: pl.BlockSpec((1, steps, width), lambda b: (b, 0, col0 // width))
    new = lambda: per_seq(ATTN_HEADS, qrows, ATTN_DH)
    cache = lambda: per_seq(ATTN_HEADS, ATTN_DH, wb)
    weights = 2 * (d * d + 3 * d * dff)
    work = tm * (6 * d * 4 + 2 * dff * 4 + dff * 2 + d * 2)
    caches = 2 * 2 * ATTN_W * wb * 4
    state = 4 * MLSTM_HEADS * MLSTM_DH * MLSTM_DH * 4
    return pl.pallas_call(
        functools.partial(_ffn_and_sample_mixers_body, steps=steps, wb=wb),
        grid=(batch,),
        in_specs=[row(), row(), once((d, d)), once((1, d)), once((d, dff)), once((d, dff)),
                  once((dff, d)), once((1, d)),
                  new(), new(), new(), cache(), cache(),
                  zcols(W2, COL_QK), zcols(MLSTM_W, COL_VM), zcols(MLSTM_W, COL_OM),
                  per_seq(steps, LANES), per_seq(CONV_W - 1, W2),
                  per_seq(MLSTM_HEADS, MLSTM_DH, MLSTM_DH), per_seq(MLSTM_HEADS, MLSTM_DH),
                  per_seq(1, LANES),
                  once((CONV_W, W2)), once((1, W2)), once((1, LANES)), once((1, MLSTM_W))],
        out_specs=[row(), new(),
                   per_seq(steps, MLSTM_W), per_seq(CONV_W - 1, W2),
                   per_seq(MLSTM_HEADS, MLSTM_DH, MLSTM_DH), per_seq(MLSTM_HEADS, MLSTM_DH),
                   per_seq(MLSTM_HEADS, LANES)],
        out_shape=[jax.ShapeDtypeStruct((n, d), F32),
                   jax.ShapeDtypeStruct((batch, ATTN_HEADS, qrows, ATTN_DH), F32),
                   jax.ShapeDtypeStruct((batch, steps, MLSTM_W), F32),
                   jax.ShapeDtypeStruct((batch, CONV_W - 1, W2), F32),
                   jax.ShapeDtypeStruct((batch, MLSTM_HEADS, MLSTM_DH, MLSTM_DH), F32),
                   jax.ShapeDtypeStruct((batch, MLSTM_HEADS, MLSTM_DH), F32),
                   jax.ShapeDtypeStruct((batch, MLSTM_HEADS, LANES), F32)],
        scratch_shapes=[pltpu.VMEM((ATTN_W, LANES), F32),
                        pltpu.VMEM((ATTN_HEADS, qrows, ATTN_DH), F32),
                        pltpu.VMEM((ATTN_HEADS, qrows, ATTN_DH), F32)],
        compiler_params=_cparams(("arbitrary",), (weights + work + caches + state) // MIB + 10),
        name="ffn_and_sample_mixers",
    )(x1, o, w_mo, g_ffn.reshape(1, d), w_gate, w_up, w_down, g_final.reshape(1, d),
      q8, kn8, vn8, kc_t, vc_t,
      z3, z3, z3, gates3, conv, c0, n0, m0, conv_w, conv_b, bg_row, g_mh)


def _xattn_sample_body(q_ref, k_ref, v_ref, o_ref):
    Q = q_ref.shape[1]
    mem_len = k_ref.shape[1]
    rows = mem_len * MEM_HEADS
    for g in range(q_ref.shape[0]):
        k2 = k_ref[g].reshape(rows, MEM_DH).astype(BF16)
        v2 = v_ref[g].reshape(rows, MEM_DH).astype(BF16)
        q_all = q_ref[g]
        qs = jnp.concatenate([q_all[:, h * MEM_DH:(h + 1) * MEM_DH] for h in range(MEM_HEADS)], axis=0)
        s = _dot_nt(qs, k2) * (MEM_DH ** -0.5)
        q_head = lax.div(lax.broadcasted_iota(jnp.int32, s.shape, 0), Q)
        k_head = lax.rem(lax.broadcasted_iota(jnp.int32, s.shape, 1), MEM_HEADS)
        s = jnp.where(q_head == k_head, s, MASKED)
        e = jnp.exp(s - jnp.max(s, axis=1, keepdims=True))
        p = e / jnp.sum(e, axis=1, keepdims=True)
        o = _dot(p.astype(BF16), v2)
        for h in range(MEM_HEADS):
            o_ref[g, :, h * MEM_DH:(h + 1) * MEM_DH] = o[h * Q:(h + 1) * Q, :].astype(BF16)


def _xattn_sample(q8, mem_k, mem_v, *, group):
    batch, qrows, d = q8.shape
    mem_len = mem_k.shape[1]
    mem = lambda: pl.BlockSpec((group, mem_len, MEM_HEADS, MEM_DH), lambda b: (b, 0, 0, 0))
    return pl.pallas_call(
        _xattn_sample_body,
        grid=(batch // group,),
        in_specs=[pl.BlockSpec((group, qrows, d), lambda b: (b, 0, 0)), mem(), mem()],
        out_specs=pl.BlockSpec((group, qrows, d), lambda b: (b, 0, 0)),
        out_shape=jax.ShapeDtypeStruct((batch, qrows, d), BF16),
        compiler_params=_cparams(("arbitrary",), 32),
        name="xattn_sample",
    )(q8, mem_k, mem_v)


def _pad_rows(a, rows):
    return jnp.pad(a, ((0, 0), (0, rows - a.shape[1]), (0, 0)))


def kernel(x_prompt, x_sample, mem_prompt, cache_attn_k, cache_attn_v, cache_mem_k, cache_mem_v, state_conv, state_C, state_n, state_m, g_mix, w_in, conv_w, conv_b, b_gates, g_mh, w_out, g_mem, w_mk, w_mv, g_xattn, w_mq, w_mo, g_ffn, w_gate, w_up, w_down, g_final):
    B, T, D = x_prompt.shape
    SB, ST, _ = x_sample.shape
    mem_len = mem_prompt.shape[1]
    W2 = 2 * MLSTM_W
    gate0 = COL_QA
    n_gate = 2 * MLSTM_HEADS

    w_in_t = w_in.T
    w_main_t = jnp.concatenate([w_in_t[:gate0], w_in_t[gate0 + n_gate:]], axis=0).astype(BF16)
    w_gate_t = jnp.pad(w_in_t[gate0:gate0 + n_gate], ((0, LANES - n_gate), (0, 0))).astype(BF16)
    bg_row = jnp.pad(b_gates, (0, LANES - n_gate)).reshape(1, LANES)
    w_mem = jnp.concatenate([w_mk, w_mv], axis=1).astype(BF16)
    w_out_m = w_out[:MLSTM_W].astype(BF16)
    w_out_a = w_out[MLSTM_W:].astype(BF16)
    w_mq_b, w_mo_b = w_mq.astype(BF16), w_mo.astype(BF16)
    w_gate_b, w_up_b, w_down_b = w_gate.astype(BF16), w_up.astype(BF16), w_down.astype(BF16)
    conv_b2 = conv_b.reshape(1, W2)
    g_mh2 = g_mh.reshape(1, MLSTM_W)

    xp = x_prompt.reshape(B * T, D)
    zmem = _norm_matmul(mem_prompt.reshape(B * mem_len, D), g_mem, w_mem, tm=B * mem_len, tn=512)
    z, gates = _inproj(xp, g_mix, w_main_t, w_gate_t, tm=512)
    hm, p_conv, p_C, p_n, p_m = _mlstm_prompt(z, gates, conv_w, conv_b2, bg_row, g_mh2,
                                              batch=B, seq=T, chunk=128, nchunk=4)
    ha = _dilated_prompt(z, batch=B, seq=T, tq=2048)
    x1, o = _outproj_xattn(xp, hm, ha, w_out_m, w_out_a, g_xattn, w_mq_b, zmem,
                           seq=T, mem_len=mem_len, tm=512)

    keep = min(DILATED[-1][0], T)
    z3p = z.reshape(B, T, N_MAIN)
    p_attn_k = z3p[:, T - keep:, COL_KA:COL_KA + ATTN_W].reshape(B, keep, ATTN_HEADS, ATTN_DH)
    p_attn_v = z3p[:, T - keep:, COL_VA:COL_VA + ATTN_W].reshape(B, keep, ATTN_HEADS, ATTN_DH)
    p_mem_k = zmem[:, :D].reshape(B, mem_len, MEM_HEADS, MEM_DH)
    p_mem_v = zmem[:, D:].reshape(B, mem_len, MEM_HEADS, MEM_DH)

    xs = x_sample.reshape(SB * ST, D)
    zs, gates_s = _inproj(xs, g_mix, w_main_t, w_gate_t, tm=512)
    zs3 = zs.reshape(SB, ST, N_MAIN)
    m0 = jnp.pad(state_m, ((0, 0), (0, LANES - MLSTM_HEADS))).reshape(SB, 1, LANES)
    qrows = 8

    def heads_first(col0):
        a = zs3[:, :, col0:col0 + ATTN_W].reshape(SB, ST, ATTN_HEADS, ATTN_DH).transpose(0, 2, 1, 3)
        return jnp.pad(a, ((0, 0), (0, 0), (0, qrows - ST), (0, 0)))

    y_prompt, ha_s, hm_s, s_conv, s_C, s_n, s_m = _ffn_and_sample_mixers(
        x1, o, w_mo_b, g_ffn, w_gate_b, w_up_b, w_down_b, g_final,
        heads_first(COL_QA), heads_first(COL_KA), heads_first(COL_VA),
        jnp.transpose(cache_attn_k, (0, 2, 3, 1)), jnp.transpose(cache_attn_v, (0, 2, 3, 1)),
        zs3, gates_s.reshape(SB, ST, LANES), state_conv, state_C, state_n, m0,
        conv_w, conv_b2, bg_row, g_mh2, steps=ST)
    ha_s = ha_s[:, :, :ST].transpose(0, 2, 1, 3)
    x1s, qs = _outproj_q(xs, hm_s.reshape(SB * ST, MLSTM_W), ha_s.reshape(SB * ST, ATTN_W),
                         w_out_m, w_out_a, g_xattn, w_mq_b, tm=SB * ST)
    os_ = _xattn_sample(_pad_rows(qs.reshape(SB, ST, D), qrows), cache_mem_k, cache_mem_v, group=4)[:, :ST]
    y_sample = _post_ffn(x1s, os_.reshape(SB * ST, D), w_mo_b, g_ffn, w_gate_b, w_up_b, w_down_b,
                         g_final, tm=512)

    s_attn_k = zs3[:, :, COL_KA:COL_KA + ATTN_W].reshape(SB, ST, ATTN_HEADS, ATTN_DH)
    s_attn_v = zs3[:, :, COL_VA:COL_VA + ATTN_W].reshape(SB, ST, ATTN_HEADS, ATTN_DH)

    return (y_prompt.reshape(B, T, D), y_sample.reshape(SB, ST, D), p_attn_k, p_attn_v,
            p_conv, p_C, p_n, p_m[:, :, 0], p_mem_k, p_mem_v,
            s_attn_k, s_attn_v, s_conv, s_C, s_n, s_m[:, :, 0])
```
